```python
import functools
import jax, jax.numpy as jnp
from jax import lax
import numpy as np

D_MODEL = 1024
BATCH = 1
SEQ = 16384
DEPTH = 1
DEC_BATCH = 32
DEC_SEQ = 1
PAST_LEN = 16384
PAGE_SIZE = 128

N_HEADS = 16
HEAD_DIM = 64
ATTN_W = N_HEADS * HEAD_DIM
Q_BLOCK = 128
D_CONV = 1024
CONV_W = 3
N_EXPERTS = 256
TOP_K = 8
N_GROUPS = 8
TOPK_GROUPS = 4
D_EXPERT = 256
D_SHARED = 256
ROUTE_SCALE = 2.5
EXP_BLOCK = 128
N_MOD = 6
EPS = 1e-6
IN_SIZES = (ATTN_W, ATTN_W, ATTN_W, N_HEADS, D_CONV, D_CONV, D_CONV, D_MODEL, D_MODEL)
D_IN = 3 * ATTN_W + N_HEADS + 3 * D_CONV + 2 * D_MODEL

kernel_name = 'fox_shortconv_moe_hybrid_step'


def rmsnorm(x, g):
    xf = x.astype(jnp.float32)
    y = xf * lax.rsqrt(jnp.mean(xf * xf, axis=-1, keepdims=True) + EPS)
    return (y * g.astype(jnp.float32)).astype(x.dtype)


def adaln_params(c, w_ada, b_ada):
    mod = jax.nn.silu(c) @ w_ada + b_ada
    return jnp.split(mod[:, None, :], N_MOD, axis=-1)


def swiglu(x, wg, wu, wd):
    return (jax.nn.silu(x @ wg) * (x @ wu)) @ wd


def split_projection(h, w_in, forget_bias):
    b, s, _ = h.shape
    offs = np.cumsum(IN_SIZES)[:-1].tolist()
    q, k, v, f, gate_out, gate_in, xc, g_attn, g_conv = jnp.split(h @ w_in, offs, axis=-1)
    heads = lambda t: t.reshape(b, s, N_HEADS, HEAD_DIM)
    logf = jax.nn.log_sigmoid((f + forget_bias).astype(jnp.float32)).astype(h.dtype)
    return heads(q), heads(k), heads(v), logf, gate_out, gate_in * xc, g_attn, g_conv


def short_conv(buf, u, conv_w):
    t = u.shape[1]
    full = jnp.concatenate([buf, u], axis=1)
    y = full[:, 0:t] * conv_w[0]
    for i in range(1, CONV_W):
        y = y + full[:, i:i + t] * conv_w[i]
    return y, full[:, t:]


def prompt_forgetting_attention(q, k, v, logf):
    b, s = q.shape[:2]
    n_blk = s // Q_BLOCK
    scale = HEAD_DIM ** -0.5
    F = jnp.cumsum(logf.astype(jnp.float32), axis=1).transpose(0, 2, 1)
    vf = v.astype(jnp.float32)
    q_blocks = q.reshape(b, n_blk, Q_BLOCK, N_HEADS, HEAD_DIM).transpose(1, 0, 2, 3, 4)
    F_blocks = F.reshape(b, N_HEADS, n_blk, Q_BLOCK).transpose(2, 0, 1, 3)
    k_pos = jnp.arange(s)

    def one_block(args):
        i, qb, Fq = args
        sc = jnp.einsum('bqhd,bkhd->bhqk', qb, k, preferred_element_type=jnp.float32) * scale
        sc = sc + Fq[..., :, None] - F[:, :, None, :]
        q_pos = i * Q_BLOCK + jnp.arange(Q_BLOCK)
        sc = jnp.where(k_pos[None, :] <= q_pos[:, None], sc, -jnp.inf)
        p = jax.nn.softmax(sc, axis=-1)
        return jnp.einsum('bhqk,bkhd->bqhd', p, vf)

    o = lax.map(one_block, (jnp.arange(n_blk), q_blocks, F_blocks))
    return o.transpose(1, 0, 2, 3, 4).reshape(b, s, N_HEADS, HEAD_DIM).astype(q.dtype)


def sample_forgetting_attention(q, k, v, logf, cache_k, cache_v, cache_logf, page_table, layer):
    bd, t = q.shape[:2]
    n_pages = page_table.shape[1]
    past = n_pages * PAGE_SIZE
    scale = HEAD_DIM ** -0.5
    lf_past = cache_logf[layer, page_table].reshape(bd, past, N_HEADS)
    F = jnp.cumsum(jnp.concatenate([lf_past, logf], axis=1).astype(jnp.float32), axis=1)
    F = F.transpose(0, 2, 1)
    Fq = F[:, :, past:]
    F_pages = F[:, :, :past].reshape(bd, N_HEADS, n_pages, PAGE_SIZE).transpose(2, 0, 1, 3)
    s_new = jnp.einsum('bqhd,bkhd->bhqk', q, k, preferred_element_type=jnp.float32) * scale
    s_new = s_new + Fq[..., :, None] - Fq[..., None, :]
    s_new = jnp.where(jnp.tril(jnp.ones((t, t), bool)), s_new, -jnp.inf)
    m0 = s_new.max(-1)
    p0 = jnp.exp(s_new - m0[..., None])
    l0 = p0.sum(-1)
    acc0 = jnp.einsum('bhqk,bkhd->bhqd', p0, v.astype(jnp.float32))

    def page_step(carry, xs):
        m, l, acc = carry
        pidx, Fk = xs
        kp = cache_k[layer, pidx]
        vp = cache_v[layer, pidx]
        sc = jnp.einsum('bqhd,bkhd->bhqk', q, kp, preferred_element_type=jnp.float32) * scale
        sc = sc + Fq[..., :, None] - Fk[:, :, None, :]
        m_new = jnp.maximum(m, sc.max(-1))
        p = jnp.exp(sc - m_new[..., None])
        corr = jnp.exp(m - m_new)
        acc = acc * corr[..., None] + jnp.einsum('bhqk,bkhd->bhqd', p, vp.astype(jnp.float32))
        return (m_new, l * corr + p.sum(-1), acc), None

    (m, l, acc), _ = lax.scan(page_step, (m0, l0, acc0), (page_table.T, F_pages))
    out = acc / l[..., None]
    return out.transpose(0, 2, 1, 3).astype(q.dtype)


def route(h, w_router, router_bias):
    n = h.shape[0]
    scores = jax.nn.sigmoid((h @ w_router).astype(jnp.float32))
    biased = scores + router_bias.astype(jnp.float32)
    grp = biased.reshape(n, N_GROUPS, N_EXPERTS // N_GROUPS)
    grp_score = lax.top_k(grp, 2)[0].sum(-1)
    _, top_grp = lax.top_k(grp_score, TOPK_GROUPS)
    grp_mask = jax.nn.one_hot(top_grp, N_GROUPS, dtype=jnp.float32).sum(-2)
    exp_mask = jnp.repeat(grp_mask, N_EXPERTS // N_GROUPS, axis=-1) > 0
    _, topi = lax.top_k(jnp.where(exp_mask, biased, -jnp.inf), TOP_K)
    w = jnp.take_along_axis(scores, topi, axis=-1)
    w = ROUTE_SCALE * w / w.sum(-1, keepdims=True)
    return topi.astype(jnp.int32), w


def routed_experts(h, topi, topw, w_gate, w_up, w_down):
    n, d = h.shape
    nk = n * TOP_K
    n_blocks = -(-nk // EXP_BLOCK) + N_EXPERTS
    eid = topi.reshape(nk)
    tok = jnp.arange(nk, dtype=jnp.int32) // TOP_K
    wts = topw.reshape(nk).astype(h.dtype)
    order = jnp.argsort(eid)
    eid_s, tok_s, w_s = eid[order], tok[order], wts[order]
    counts = jnp.bincount(eid, length=N_EXPERTS)
    nblk_e = (counts + EXP_BLOCK - 1) // EXP_BLOCK
    blk_end = jnp.cumsum(nblk_e)
    blk_start = blk_end - nblk_e
    row_start = jnp.cumsum(counts) - counts
    dest = blk_start[eid_s] * EXP_BLOCK + (jnp.arange(nk, dtype=jnp.int32) - row_start[eid_s])
    rows = n_blocks * EXP_BLOCK
    row_tok = jnp.zeros((rows,), jnp.int32).at[dest].set(tok_s)
    row_w = jnp.zeros((rows,), h.dtype).at[dest].set(w_s)
    block_exp = jnp.minimum(jnp.searchsorted(blk_end, jnp.arange(n_blocks), side='right'), N_EXPERTS - 1)

    def one_block(args):
        e, toks = args
        return swiglu(h[toks], w_gate[e], w_up[e], w_down[e])

    yb = lax.map(one_block, (block_exp, row_tok.reshape(n_blocks, EXP_BLOCK)))
    y = yb.reshape(rows, d) * row_w[:, None]
    return jnp.zeros_like(h).at[row_tok].add(y)


def moe_ffn(h, w_router, router_bias, w_sh_gate, w_sh_up, w_sh_down, w_ex_gate, w_ex_up, w_ex_down):
    b, s, d = h.shape
    hf = h.reshape(b * s, d)
    topi, topw = route(hf, w_router, router_bias)
    y = swiglu(hf, w_sh_gate, w_sh_up, w_sh_down) + routed_experts(hf, topi, topw, w_ex_gate, w_ex_up, w_ex_down)
    return y.reshape(b, s, d)


def decoder_layer(x, c, attend, conv_buf, w_ada, b_ada, norm_mix, w_in, forget_bias, conv_w, w_out,
                  norm_ffn, w_router, router_bias, w_sh_gate, w_sh_up, w_sh_down, w_ex_gate, w_ex_up, w_ex_down):
    b, s, _ = x.shape
    sh1, sc1, g1, sh2, sc2, g2 = adaln_params(c, w_ada, b_ada)
    h = rmsnorm(x, norm_mix) * (1 + sc1) + sh1
    q, k, v, logf, gate_out, u, g_attn, g_conv = split_projection(h, w_in, forget_bias)
    y_attn = attend(q, k, v, logf).reshape(b, s, ATTN_W)
    conv_y, new_buf = short_conv(conv_buf, u, conv_w)
    y_conv = gate_out * conv_y
    merged = jax.nn.sigmoid(g_attn) * y_attn + jax.nn.sigmoid(g_conv) * y_conv
    x = x + g1 * (merged @ w_out)
    h2 = rmsnorm(x, norm_ffn) * (1 + sc2) + sh2
    x = x + g2 * moe_ffn(h2, w_router, router_bias, w_sh_gate, w_sh_up, w_sh_down, w_ex_gate, w_ex_up, w_ex_down)
    return x, k, v, logf, new_buf


def setup_inputs(seed: int = 0) -> dict:
    key = jax.random.key(seed)
    ks = jax.random.split(key, 26)
    n_pages = PAST_LEN // PAGE_SIZE
    in_use = DEC_BATCH * n_pages
    n_phys = in_use + max(1, in_use // 4)
    nrm = lambda k, shape, s=1.0: s * jax.random.normal(k, shape, jnp.float32)
    page_table = jax.random.permutation(ks[0], n_phys)[:in_use].reshape(DEC_BATCH, n_pages).astype(jnp.int32)
    return {
        'x_prompt': nrm(ks[1], (BATCH, SEQ, D_MODEL)),
        'x_sample': nrm(ks[2], (DEC_BATCH, DEC_SEQ, D_MODEL)),
        'cache_k': nrm(ks[3], (DEPTH, n_phys, PAGE_SIZE, N_HEADS, HEAD_DIM)),
        'cache_v': nrm(ks[4], (DEPTH, n_phys, PAGE_SIZE, N_HEADS, HEAD_DIM)),
        'cache_logf': jax.nn.log_sigmoid(3.0 + nrm(ks[5], (DEPTH, n_phys, PAGE_SIZE, N_HEADS))),
        'state_conv': nrm(ks[6], (DEPTH, DEC_BATCH, CONV_W - 1, D_CONV)),
        'page_table': page_table,
        'c_prompt': nrm(ks[7], (BATCH, D_MODEL)),
        'c_sample': nrm(ks[8], (DEC_BATCH, D_MODEL)),
        'w_ada': nrm(ks[9], (DEPTH, D_MODEL, N_MOD * D_MODEL), 0.5 * D_MODEL ** -0.5),
        'b_ada': nrm(ks[10], (DEPTH, N_MOD * D_MODEL), 0.02),
        'norm_mix': 1.0 + nrm(ks[11], (DEPTH, D_MODEL), 0.02),
        'w_in': nrm(ks[12], (DEPTH, D_MODEL, D_IN), D_MODEL ** -0.5),
        'forget_bias': 2.0 + 3.0 * jax.random.uniform(ks[13], (DEPTH, N_HEADS), jnp.float32),
        'conv_w': nrm(ks[14], (DEPTH, CONV_W, D_CONV), CONV_W ** -0.5),
        'w_out': nrm(ks[15], (DEPTH, D_MODEL, D_MODEL), D_MODEL ** -0.5),
        'norm_ffn': 1.0 + nrm(ks[16], (DEPTH, D_MODEL), 0.02),
        'w_router': nrm(ks[17], (DEPTH, D_MODEL, N_EXPERTS), D_MODEL ** -0.5),
        'router_bias': nrm(ks[18], (DEPTH, N_EXPERTS), 0.01),
        'w_sh_gate': nrm(ks[19], (DEPTH, D_MODEL, D_SHARED), D_MODEL ** -0.5),
        'w_sh_up': nrm(ks[20], (DEPTH, D_MODEL, D_SHARED), D_MODEL ** -0.5),
        'w_sh_down': nrm(ks[21], (DEPTH, D_SHARED, D_MODEL), D_SHARED ** -0.5),
        'w_ex_gate': nrm(ks[22], (DEPTH, N_EXPERTS, D_MODEL, D_EXPERT), D_MODEL ** -0.5),
        'w_ex_up': nrm(ks[23], (DEPTH, N_EXPERTS, D_MODEL, D_EXPERT), D_MODEL ** -0.5),
        'w_ex_down': nrm(ks[24], (DEPTH, N_EXPERTS, D_EXPERT, D_MODEL), D_EXPERT ** -0.5),
        'norm_final': 1.0 + nrm(ks[25], (D_MODEL,), 0.02),
    }


def reference(x_prompt, x_sample, cache_k, cache_v, cache_logf, state_conv, page_table, c_prompt, c_sample,
              w_ada, b_ada, norm_mix, w_in, forget_bias, conv_w, w_out, norm_ffn, w_router, router_bias,
              w_sh_gate, w_sh_up, w_sh_down, w_ex_gate, w_ex_up, w_ex_down, norm_final):
    xp, xs = x_prompt, x_sample
    kp_l, vp_l, lfp_l, cp_l, ks_l, vs_l, lfs_l, cs_l = [], [], [], [], [], [], [], []
    for l in range(DEPTH):
        layer_w = (w_ada[l], b_ada[l], norm_mix[l], w_in[l], forget_bias[l], conv_w[l], w_out[l], norm_ffn[l],
                   w_router[l], router_bias[l], w_sh_gate[l], w_sh_up[l], w_sh_down[l],
                   w_ex_gate[l], w_ex_up[l], w_ex_down[l])
        attend_sample = functools.partial(sample_forgetting_attention, cache_k=cache_k, cache_v=cache_v,
                                          cache_logf=cache_logf, page_table=page_table, layer=l)
        buf0 = jnp.zeros((xp.shape[0], CONV_W - 1, D_CONV), xp.dtype)
        xp, kp, vp, lfp, cp = decoder_layer(xp, c_prompt, prompt_forgetting_attention, buf0, *layer_w)
        xs, k_s, v_s, lfs, cs = decoder_layer(xs, c_sample, attend_sample, state_conv[l], *layer_w)
        kp_l.append(kp); vp_l.append(vp); lfp_l.append(lfp); cp_l.append(cp)
        ks_l.append(k_s); vs_l.append(v_s); lfs_l.append(lfs); cs_l.append(cs)
    y_prompt = rmsnorm(xp, norm_final)
    y_sample = rmsnorm(xs, norm_final)
    return (y_prompt, y_sample,
            jnp.stack(kp_l), jnp.stack(vp_l), jnp.stack(lfp_l), jnp.stack(cp_l),
            jnp.stack(ks_l), jnp.stack(vs_l), jnp.stack(lfs_l), jnp.stack(cs_l))
```

```python
import functools

import jax
import jax.numpy as jnp
from jax import lax
from jax.experimental import pallas as pl
from jax.experimental.pallas import tpu as pltpu

F32 = jnp.float32
BF16 = jnp.bfloat16
I32 = jnp.int32

D_MODEL = 1024
N_HEADS = 16
HEAD_DIM = 64
PAGE = 128
CONV_W = 3
N_EXPERTS = 256
TOP_K = 8
N_GROUPS = 8
GROUP_SIZE = N_EXPERTS // N_GROUPS
TOPK_GROUPS = 4
D_EXPERT = 256
N_MOD = 6
EPS = 1e-6
ROUTE_SCALE = 2.5
LOG2E = 1.4426950408889634
QK_SCALE = HEAD_DIM ** -0.5

LANES = 128
SUBLANES = 8
ROW_CHUNKS = D_MODEL // LANES
VMEM_LIMIT = 48 * 1024 * 1024

NEG_INF = float("-inf")


def _cparams(*sem):
    return pltpu.CompilerParams(dimension_semantics=sem, vmem_limit_bytes=VMEM_LIMIT)


def _split3(x):
    hi = x.astype(BF16)
    r = x - hi.astype(F32)
    mid = r.astype(BF16)
    lo = (r - mid.astype(F32)).astype(BF16)
    return hi, mid, lo


def _dot(a, b):
    return jnp.dot(a, b, preferred_element_type=F32)


def _dot_nt(a, b):
    return lax.dot_general(a, b, (((1,), (1,)), ((), ())), preferred_element_type=F32)


def _dot_exact_rhs(a01, x):
    hi, mid, lo = _split3(x)
    return _dot(a01, hi) + _dot(a01, mid) + _dot(a01, lo)


def _dot_exact_lhs(x, b01):
    hi, mid, lo = _split3(x)
    return _dot(hi, b01) + _dot(mid, b01) + _dot(lo, b01)


def _silu(x):
    return x * jax.nn.sigmoid(x)


def _log_sigmoid(z):
    return jnp.minimum(z, 0.0) - jnp.log1p(jnp.exp(-jnp.abs(z)))


def _rms_mod(x, g, mul, shift):
    ms = jnp.mean(x * x, axis=-1, keepdims=True)
    return (x * lax.rsqrt(ms + EPS) * g) * mul + shift


def _mod_spec(per_row, tm):
    if per_row:
        return pl.BlockSpec((tm, D_MODEL), lambda i: (i, 0))
    return pl.BlockSpec((1, D_MODEL), lambda i: (0, 0))


def _const_spec(shape):
    nd = len(shape)
    return pl.BlockSpec(shape, lambda i: (0,) * nd)


def _adaln_kernel(c_ref, w_ref, b_ref, o_ref):
    a = _silu(c_ref[...]).astype(BF16)
    o_ref[...] = _dot(a, w_ref[...].astype(BF16)) + b_ref[...]


def _adaln(c_all, w_ada, b_ada):
    r = c_all.shape[0]
    n = w_ada.shape[1]
    tn = 1536
    return pl.pallas_call(
        _adaln_kernel,
        grid=(n // tn,),
        in_specs=[pl.BlockSpec((r, D_MODEL), lambda j: (0, 0)),
                  pl.BlockSpec((D_MODEL, tn), lambda j: (0, j)),
                  pl.BlockSpec((1, tn), lambda j: (0, j))],
        out_specs=pl.BlockSpec((r, tn), lambda j: (0, j)),
        out_shape=jax.ShapeDtypeStruct((r, n), F32),
        compiler_params=_cparams("arbitrary"),
        name="adaln",
    )(c_all, w_ada, b_ada.reshape(1, n))


def _qkv_prompt_kernel(x_ref, mul_ref, sh_ref, g_ref, wqt_ref, wk_ref, wv_ref, wvt_ref, wf_ref, fb_ref,
                       k_ref, v_ref, lf_ref, kb_ref, qt_ref, vt_ref, fa_ref, tri_ref, carry_ref, *, tm):
    i = pl.program_id(0)

    @pl.when(i == 0)
    def _():
        r = lax.broadcasted_iota(I32, (tm, tm), 0)
        c = lax.broadcasted_iota(I32, (tm, tm), 1)
        tri_ref[...] = jnp.where(c <= r, 1.0, 0.0).astype(BF16)
        carry_ref[...] = jnp.zeros_like(carry_ref)

    hb = _rms_mod(x_ref[...], g_ref[...], mul_ref[...], sh_ref[...]).astype(BF16)
    k = _dot(hb, wk_ref[...])
    k_ref[...] = k
    kb_ref[...] = k.astype(BF16)
    v_ref[...] = _dot(hb, wv_ref[...])
    qt_ref[...] = (_dot_nt(wqt_ref[...], hb) * (QK_SCALE * LOG2E)).astype(BF16)
    vt = _dot_nt(wvt_ref[...], hb).astype(BF16)
    for h in range(N_HEADS):
        vt_ref[h, 0] = vt[h * HEAD_DIM:(h + 1) * HEAD_DIM, :]
    lf = _log_sigmoid(_dot(hb, wf_ref[...]) + fb_ref[...])
    lf_ref[...] = lf[:, :N_HEADS]
    cum = _dot_exact_rhs(tri_ref[...], lf) + carry_ref[0:1, :]
    carry_ref[0:1, :] = cum[tm - 1:tm, :]
    hi, mid, lo = _split3(cum * (-LOG2E))
    lane = lax.broadcasted_iota(I32, (tm, LANES), 1)
    zero = jnp.zeros_like(hi)
    fa_ref[...] = jnp.where(lane < N_HEADS, hi,
                            jnp.where(lane < 2 * N_HEADS, mid, jnp.where(lane < 3 * N_HEADS, lo, zero)))


def _qkv_prompt(x, mul, sh, g, wqt, wk, wv, wvt, wf3, fb3, *, tm):
    s = x.shape[0]
    nt = s // tm
    row = pl.BlockSpec((tm, D_MODEL), lambda i: (i, 0))
    wspec = _const_spec((D_MODEL, D_MODEL))
    return pl.pallas_call(
        functools.partial(_qkv_prompt_kernel, tm=tm),
        grid=(nt,),
        in_specs=[row, _mod_spec(False, tm), _mod_spec(False, tm), _const_spec((1, D_MODEL)),
                  wspec, wspec, wspec, wspec, _const_spec((D_MODEL, LANES)), _const_spec((1, LANES))],
        out_specs=[row, row,
                   pl.BlockSpec((tm, N_HEADS), lambda i: (i, 0)),
                   row,
                   pl.BlockSpec((D_MODEL, tm), lambda i: (0, i)),
                   pl.BlockSpec((N_HEADS, 1, HEAD_DIM, tm), lambda i: (0, i, 0, 0)),
                   pl.BlockSpec((tm, LANES), lambda i: (i, 0))],
        out_shape=[jax.ShapeDtypeStruct((s, D_MODEL), F32),
                   jax.ShapeDtypeStruct((s, D_MODEL), F32),
                   jax.ShapeDtypeStruct((s, N_HEADS), F32),
                   jax.ShapeDtypeStruct((s, D_MODEL), BF16),
                   jax.ShapeDtypeStruct((D_MODEL, s), BF16),
                   jax.ShapeDtypeStruct((N_HEADS, nt, HEAD_DIM, tm), BF16),
                   jax.ShapeDtypeStruct((s, LANES), BF16)],
        scratch_shapes=[pltpu.VMEM((tm, tm), BF16), pltpu.VMEM((SUBLANES, LANES), F32)],
        compiler_params=_cparams("arbitrary"),
        name="qkv_prompt",
    )(x, mul, sh, g, wqt, wk, wv, wvt, wf3, fb3)


def _qkv_sample_kernel(x_ref, mul_ref, sh_ref, g_ref, wq_ref, wk_ref, wv_ref, wf_ref, fb_ref,
                       q_ref, k_ref, v_ref, lf_ref):
    hb = _rms_mod(x_ref[...], g_ref[...], mul_ref[...], sh_ref[...]).astype(BF16)
    q_ref[...] = _dot(hb, wq_ref[...]) * QK_SCALE
    k_ref[...] = _dot(hb, wk_ref[...])
    v_ref[...] = _dot(hb, wv_ref[...])
    lf_ref[...] = _log_sigmoid(_dot(hb, wf_ref[...]) + fb_ref[...])


def _qkv_sample(x, mul, sh, g, wq, wk, wv, wf3, fb3):
    n = x.shape[0]
    row = _const_spec((n, D_MODEL))
    wspec = _const_spec((D_MODEL, D_MODEL))
    return pl.pallas_call(
        _qkv_sample_kernel,
        grid=(1,),
        in_specs=[row, row, row, _const_spec((1, D_MODEL)), wspec, wspec, wspec,
                  _const_spec((D_MODEL, LANES)), _const_spec((1, LANES))],
        out_specs=[row, row, row, _const_spec((n, LANES))],
        out_shape=[jax.ShapeDtypeStruct((n, D_MODEL), F32)] * 3 + [jax.ShapeDtypeStruct((n, LANES), F32)],
        compiler_params=_cparams("arbitrary"),
        name="qkv_sample",
    )(x, mul, sh, g, wq, wk, wv, wf3, fb3)


def _gates(hb, wgo_ref, wgi_ref, wxc_ref, wga_ref, wgc_ref):
    go = _dot(hb, wgo_ref[...])
    u = _dot(hb, wgi_ref[...]) * _dot(hb, wxc_ref[...])
    a = jax.nn.sigmoid(_dot(hb, wga_ref[...]))
    gc = jax.nn.sigmoid(_dot(hb, wgc_ref[...])) * go
    return u, a, gc


def _gate_prompt_kernel(x_ref, mul_ref, sh_ref, g_ref, wgo_ref, wgi_ref, wxc_ref, wga_ref, wgc_ref,
                        cw_ref, buf_ref, a_ref, yc_ref, tail_ref, ubuf_ref, *, tm):
    i = pl.program_id(0)

    @pl.when(i == 0)
    def _():
        ubuf_ref[0:SUBLANES, :] = buf_ref[...]

    hb = _rms_mod(x_ref[...], g_ref[...], mul_ref[...], sh_ref[...]).astype(BF16)
    u, a, gc = _gates(hb, wgo_ref, wgi_ref, wxc_ref, wga_ref, wgc_ref)
    ubuf_ref[SUBLANES:SUBLANES + tm, :] = u
    conv = (ubuf_ref[SUBLANES - 2:SUBLANES - 2 + tm, :] * cw_ref[0:1, :]
            + ubuf_ref[SUBLANES - 1:SUBLANES - 1 + tm, :] * cw_ref[1:2, :]
            + u * cw_ref[2:3, :])
    a_ref[...] = a
    yc_ref[...] = gc * conv
    last = u[tm - SUBLANES:tm, :]
    ubuf_ref[0:SUBLANES, :] = last
    tail_ref[...] = last


def _gate_prompt(x, mul, sh, g, wgo, wgi, wxc, wga, wgc, cw8, buf8, *, tm):
    s = x.shape[0]
    row = pl.BlockSpec((tm, D_MODEL), lambda i: (i, 0))
    wspec = _const_spec((D_MODEL, D_MODEL))
    small = _const_spec((SUBLANES, D_MODEL))
    return pl.pallas_call(
        functools.partial(_gate_prompt_kernel, tm=tm),
        grid=(s // tm,),
        in_specs=[row, _mod_spec(False, tm), _mod_spec(False, tm), _const_spec((1, D_MODEL)),
                  wspec, wspec, wspec, wspec, wspec, small, small],
        out_specs=[row, row, small],
        out_shape=[jax.ShapeDtypeStruct((s, D_MODEL), F32),
                   jax.ShapeDtypeStruct((s, D_MODEL), F32),
                   jax.ShapeDtypeStruct((SUBLANES, D_MODEL), F32)],
        scratch_shapes=[pltpu.VMEM((tm + SUBLANES, D_MODEL), F32)],
        compiler_params=_cparams("arbitrary"),
        name="gate_prompt",
    )(x, mul, sh, g, wgo, wgi, wxc, wga, wgc, cw8, buf8)


def _gate_sample_kernel(x_ref, mul_ref, sh_ref, g_ref, wgo_ref, wgi_ref, wxc_ref, wga_ref, wgc_ref,
                        cw_ref, p2_ref, p1_ref, a_ref, yc_ref, u_ref):
    hb = _rms_mod(x_ref[...], g_ref[...], mul_ref[...], sh_ref[...]).astype(BF16)
    u, a, gc = _gates(hb, wgo_ref, wgi_ref, wxc_ref, wga_ref, wgc_ref)
    conv = p2_ref[...] * cw_ref[0:1, :] + p1_ref[...] * cw_ref[1:2, :] + u * cw_ref[2:3, :]
    a_ref[...] = a
    yc_ref[...] = gc * conv
    u_ref[...] = u


def _gate_sample(x, mul, sh, g, wgo, wgi, wxc, wga, wgc, cw8, p2, p1):
    n = x.shape[0]
    row = _const_spec((n, D_MODEL))
    wspec = _const_spec((D_MODEL, D_MODEL))
    return pl.pallas_call(
        _gate_sample_kernel,
        grid=(1,),
        in_specs=[row, row, row, _const_spec((1, D_MODEL)), wspec, wspec, wspec, wspec, wspec,
                  _const_spec((SUBLANES, D_MODEL)), row, row],
        out_specs=[row, row, row],
        out_shape=[jax.ShapeDtypeStruct((n, D_MODEL), F32)] * 3,
        compiler_params=_cparams("arbitrary"),
        name="gate_sample",
    )(x, mul, sh, g, wgo, wgi, wxc, wga, wgc, cw8, p2, p1)


def _attn_prompt_kernel(qt_ref, k_ref, fa_ref, vt_ref, o_ref, *, tq, tk):
    h = pl.program_id(0)
    i = pl.program_id(1)
    rows = lax.broadcasted_iota(I32, (2 * HEAD_DIM, tq), 0)
    mine = jnp.where(rows >= HEAD_DIM, 1, 0) == h % 2
    qm = jnp.where(mine, qt_ref[...].astype(F32), 0.0).astype(BF16)
    ones = jnp.where((rows == h) | (rows == h + N_HEADS) | (rows == h + 2 * N_HEADS), 1.0, 0.0).astype(BF16)
    qa = jnp.concatenate([qm, ones], axis=0)

    def block(kb, carry, masked):
        m, l, acc = carry
        k0 = pl.multiple_of(kb * tk, tk)
        ka = jnp.concatenate([k_ref[pl.ds(k0, tk), :], fa_ref[pl.ds(k0, tk), :]], axis=1)
        s = _dot(ka, qa)
        if masked:
            kpos = k0 + lax.broadcasted_iota(I32, (tk, tq), 0)
            qpos = i * tq + lax.broadcasted_iota(I32, (tk, tq), 1)
            s = jnp.where(kpos <= qpos, s, NEG_INF)
        m_new = jnp.maximum(m, jnp.max(s, axis=0, keepdims=True))
        alpha = jnp.exp2(m - m_new)
        p = jnp.exp2(s - m_new)
        l = alpha * l + jnp.sum(p, axis=0, keepdims=True)
        acc = alpha * acc + _dot(vt_ref[0, kb], p.astype(BF16))
        return m_new, l, acc

    init = (jnp.full((1, tq), NEG_INF, F32), jnp.zeros((1, tq), F32), jnp.zeros((HEAD_DIM, tq), F32))
    ndiag = tq // tk
    nfull = i * ndiag
    carry = lax.fori_loop(0, nfull, lambda kb, c: block(kb, c, False), init)
    for d in range(ndiag):
        carry = block(nfull + d, carry, True)
    _, l, acc = carry
    o_ref[...] = acc / l


def _attn_prompt(qt, kb, fa, vt4, *, tq):
    s = kb.shape[0]
    nkb, tk = vt4.shape[1], vt4.shape[3]
    return pl.pallas_call(
        functools.partial(_attn_prompt_kernel, tq=tq, tk=tk),
        grid=(N_HEADS, s // tq),
        in_specs=[pl.BlockSpec((2 * HEAD_DIM, tq), lambda h, i: (h // 2, i)),
                  pl.BlockSpec((s, 2 * HEAD_DIM), lambda h, i: (0, h // 2)),
                  pl.BlockSpec((s, LANES), lambda h, i: (0, 0)),
                  pl.BlockSpec((1, nkb, HEAD_DIM, tk), lambda h, i: (h, 0, 0, 0))],
        out_specs=pl.BlockSpec((HEAD_DIM, tq), lambda h, i: (h, i)),
        out_shape=jax.ShapeDtypeStruct((D_MODEL, s), F32),
        compiler_params=_cparams("arbitrary", "arbitrary"),
        name="attn_prompt",
    )(qt, kb, fa, vt4)


def _attn_sample_kernel(pt_ref, q_ref, qc_ref, kn_ref, vn_ref, lfn_ref, seg_ref, segt_ref, ck_ref, cv_ref, clf_ref,
                        o_ref, qbd_ref, m_ref, l_ref, c_ref, acc_ref, tri_ref):
    j = pl.program_id(1)
    seg = seg_ref[...]
    segt = segt_ref[...]

    @pl.when(j == 0)
    def _():
        q = q_ref[0]
        qbd_ref[...] = (seg.astype(F32) * qc_ref[0]).astype(BF16)
        qk = jnp.broadcast_to(q * kn_ref[0], (SUBLANES, D_MODEL))
        m_ref[...] = _dot_exact_lhs(qk, seg)[0:1, :]
        l_ref[...] = jnp.ones_like(l_ref)
        c_ref[...] = lfn_ref[0]
        acc_ref[...] = jnp.zeros_like(acc_ref)
        acc_ref[0:1, :] = vn_ref[0]
        r = lax.broadcasted_iota(I32, (PAGE, PAGE), 0)
        c = lax.broadcasted_iota(I32, (PAGE, PAGE), 1)
        tri_ref[...] = jnp.where(c > r, 1.0, 0.0).astype(BF16)

    lf = clf_ref[0]
    suffix = _dot_exact_rhs(tri_ref[...], lf) + c_ref[...]
    s = _dot(ck_ref[0].astype(BF16), qbd_ref[...]) + suffix
    m_old = m_ref[...]
    m_new = jnp.maximum(m_old, jnp.max(s, axis=0, keepdims=True))
    alpha = jnp.exp(m_old - m_new)
    p = jnp.exp(s - m_new)
    l_ref[...] = alpha * l_ref[...] + jnp.sum(p, axis=0, keepdims=True)
    m_ref[...] = m_new
    c_ref[...] = c_ref[...] + jnp.sum(lf, axis=0, keepdims=True)
    pe = _dot(p.astype(BF16), segt)
    contrib = (pe * cv_ref[0]).reshape(PAGE // SUBLANES, SUBLANES, D_MODEL).sum(axis=0)
    alpha_e = _dot_exact_lhs(jnp.broadcast_to(alpha, (SUBLANES, N_HEADS)), segt)
    acc_ref[...] = acc_ref[...] * alpha_e + contrib

    @pl.when(j == pl.num_programs(1) - 1)
    def _():
        l_e = _dot_exact_lhs(jnp.broadcast_to(l_ref[...], (SUBLANES, N_HEADS)), segt)[0:1, :]
        o_ref[0] = jnp.sum(acc_ref[...], axis=0, keepdims=True) / l_e


def _attn_sample(page_table, q, kn, vn, lfn, cache_k, cache_v, cache_lf):
    b, npages = page_table.shape
    n_phys = cache_k.shape[0]
    ck = cache_k.reshape(n_phys, PAGE, D_MODEL)
    cv = cache_v.reshape(n_phys, PAGE, D_MODEL)
    head_of = jnp.arange(D_MODEL, dtype=I32) // HEAD_DIM
    seg = (head_of[:, None] == jnp.arange(N_HEADS, dtype=I32)[None, :]).astype(BF16)
    per_b = lambda shape: pl.BlockSpec((1,) + shape, lambda bi, j, pt: (bi, 0, 0))
    page = lambda last: pl.BlockSpec((1, PAGE, last), lambda bi, j, pt: (pt[bi, npages - 1 - j], 0, 0))
    cst = lambda shape: pl.BlockSpec(shape, lambda bi, j, pt: (0, 0))
    grid_spec = pltpu.PrefetchScalarGridSpec(
        num_scalar_prefetch=1,
        grid=(b, npages),
        in_specs=[per_b((1, D_MODEL)), per_b((D_MODEL, 1)), per_b((1, D_MODEL)), per_b((1, D_MODEL)),
                  per_b((1, N_HEADS)),
                  cst((D_MODEL, N_HEADS)), cst((N_HEADS, D_MODEL)),
                  page(D_MODEL), page(D_MODEL), page(N_HEADS)],
        out_specs=per_b((1, D_MODEL)),
        scratch_shapes=[pltpu.VMEM((D_MODEL, N_HEADS), BF16),
                        pltpu.VMEM((1, N_HEADS), F32),
                        pltpu.VMEM((1, N_HEADS), F32),
                        pltpu.VMEM((1, N_HEADS), F32),
                        pltpu.VMEM((SUBLANES, D_MODEL), F32),
                        pltpu.VMEM((PAGE, PAGE), BF16)],
    )
    out = pl.pallas_call(
        _attn_sample_kernel,
        grid_spec=grid_spec,
        out_shape=jax.ShapeDtypeStruct((b, 1, D_MODEL), F32),
        compiler_params=_cparams("arbitrary", "arbitrary"),
        name="attn_sample",
    )(page_table, q.reshape(b, 1, D_MODEL), q.reshape(b, D_MODEL, 1), kn.reshape(b, 1, D_MODEL),
      vn.reshape(b, 1, D_MODEL),
      lfn.reshape(b, 1, N_HEADS), seg, seg.T, ck, cv, cache_lf)
    return out.reshape(b, D_MODEL)


def _store_row_chunks(ref, val, rows):
    for c in range(ROW_CHUNKS):
        ref[pl.ds(c, rows, stride=ROW_CHUNKS), :] = val[:, c * LANES:(c + 1) * LANES]


def _load_row_chunks(ref, rows, lead=None):
    parts = []
    for c in range(ROW_CHUNKS):
        idx = (pl.ds(c, rows, stride=ROW_CHUNKS), slice(None))
        parts.append(ref[idx] if lead is None else ref[(lead,) + idx])
    return jnp.concatenate(parts, axis=1)


def _post_kernel(x_ref, y_ref, a_ref, yc_ref, g1_ref, mul_ref, sh_ref, g2_ref, nf_ref, wo_ref,
                 wsg_ref, wsu_ref, wsd_ref, xmid_ref, hb_ref, hc_ref, *, tm, transposed):
    y = y_ref[...].T if transposed else y_ref[...]
    merged = a_ref[...] * y + yc_ref[...]
    x1 = x_ref[...] + g1_ref[...] * _dot(merged.astype(BF16), wo_ref[...])
    h2 = _rms_mod(x1, nf_ref[...], mul_ref[...], sh_ref[...])
    hb = h2.astype(BF16)
    act = (_silu(_dot(hb, wsg_ref[...])) * _dot(hb, wsu_ref[...])).astype(BF16)
    xmid_ref[...] = x1 + g2_ref[...] * _dot(act, wsd_ref[...])
    hb_ref[...] = hb
    _store_row_chunks(hc_ref, h2, tm)


def _post(x, y, a, yc, g1, mul, sh, g2, nf, wo, wsg, wsu, wsd, *, tm, per_row, transposed):
    n = x.shape[0]
    row = pl.BlockSpec((tm, D_MODEL), lambda i: (i, 0))
    yspec = pl.BlockSpec((D_MODEL, tm), lambda i: (0, i)) if transposed else row
    mod = _mod_spec(per_row, tm)
    return pl.pallas_call(
        functools.partial(_post_kernel, tm=tm, transposed=transposed),
        grid=(n // tm,),
        in_specs=[row, yspec, row, row, mod, mod, mod, mod, _const_spec((1, D_MODEL)),
                  _const_spec((D_MODEL, D_MODEL)), _const_spec((D_MODEL, D_EXPERT)),
                  _const_spec((D_MODEL, D_EXPERT)), _const_spec((D_EXPERT, D_MODEL))],
        out_specs=[row, row, pl.BlockSpec((tm * ROW_CHUNKS, LANES), lambda i: (i, 0))],
        out_shape=[jax.ShapeDtypeStruct((n, D_MODEL), F32),
                   jax.ShapeDtypeStruct((n, D_MODEL), BF16),
                   jax.ShapeDtypeStruct((n * ROW_CHUNKS, LANES), F32)],
        compiler_params=_cparams("arbitrary"),
        name="post_t" if transposed else "post_n",
    )(x, y, a, yc, g1, mul, sh, g2, nf, wo, wsg, wsu, wsd)


def _route_kernel(h_ref, wrt_ref, rb_ref, ti_ref, tw_ref, rk_ref, cnt_ref, tri_ref, base_ref, *, tn):
    i = pl.program_id(0)

    @pl.when(i == 0)
    def _():
        r = lax.broadcasted_iota(I32, (tn, tn), 0)
        c = lax.broadcasted_iota(I32, (tn, tn), 1)
        tri_ref[...] = jnp.where(r < c, 1.0, 0.0).astype(BF16)
        base_ref[...] = jnp.zeros_like(base_ref)

    scores = jax.nn.sigmoid(_dot_nt(wrt_ref[...], h_ref[...]))
    biased = scores + rb_ref[:, 0:1]
    gidx = lax.broadcasted_iota(I32, (GROUP_SIZE, tn), 0)
    groups = [biased[g * GROUP_SIZE:(g + 1) * GROUP_SIZE, :] for g in range(N_GROUPS)]
    gs = []
    for grp in groups:
        m1 = jnp.max(grp, axis=0, keepdims=True)
        first = jnp.min(jnp.where(grp == m1, gidx, GROUP_SIZE), axis=0, keepdims=True)
        m2 = jnp.max(jnp.where(gidx == first, NEG_INF, grp), axis=0, keepdims=True)
        gs.append(m1 + m2)
    cands = []
    for g in range(N_GROUPS):
        beaten = jnp.zeros((1, tn), I32)
        for o in range(N_GROUPS):
            if o != g:
                ahead = (gs[o] >= gs[g]) if o < g else (gs[o] > gs[g])
                beaten = beaten + jnp.where(ahead, 1, 0)
        cands.append(jnp.where(beaten < TOPK_GROUPS, groups[g], NEG_INF))
    cand = jnp.concatenate(cands, axis=0)
    eidx = lax.broadcasted_iota(I32, (N_EXPERTS, tn), 0)
    sel = jnp.zeros((N_EXPERTS, tn), F32)
    picks, weights = [], []
    for _ in range(TOP_K):
        mk = jnp.max(cand, axis=0, keepdims=True)
        ik = jnp.min(jnp.where(cand == mk, eidx, N_EXPERTS), axis=0, keepdims=True)
        hit = eidx == ik
        picks.append(ik)
        weights.append(jnp.sum(jnp.where(hit, scores, 0.0), axis=0, keepdims=True))
        sel = jnp.where(hit, 1.0, sel)
        cand = jnp.where(hit, NEG_INF, cand)
    wsum = weights[0]
    for w in weights[1:]:
        wsum = wsum + w
    before = _dot(sel.astype(BF16), tri_ref[...]) + base_ref[:, 0:1]
    for k in range(TOP_K):
        ti_ref[k:k + 1, :] = picks[k]
        tw_ref[k:k + 1, :] = ROUTE_SCALE * weights[k] / wsum
        rank = jnp.sum(jnp.where(eidx == picks[k], before, 0.0), axis=0, keepdims=True)
        rk_ref[k:k + 1, :] = rank.astype(I32)
    base_ref[...] = base_ref[...] + jnp.sum(sel, axis=1, keepdims=True)
    cnt_ref[...] = base_ref[...]


def _route(hb, wrt, rb, *, tn):
    n = hb.shape[0]
    col = pl.BlockSpec((TOP_K, tn), lambda i: (0, i))
    return pl.pallas_call(
        functools.partial(_route_kernel, tn=tn),
        grid=(n // tn,),
        in_specs=[pl.BlockSpec((tn, D_MODEL), lambda i: (i, 0)),
                  _const_spec((N_EXPERTS, D_MODEL)), _const_spec((N_EXPERTS, LANES))],
        out_specs=[col, col, col, _const_spec((N_EXPERTS, LANES))],
        out_shape=[jax.ShapeDtypeStruct((TOP_K, n), I32),
                   jax.ShapeDtypeStruct((TOP_K, n), F32),
                   jax.ShapeDtypeStruct((TOP_K, n), I32),
                   jax.ShapeDtypeStruct((N_EXPERTS, LANES), F32)],
        scratch_shapes=[pltpu.VMEM((tn, tn), BF16), pltpu.VMEM((N_EXPERTS, LANES), F32)],
        compiler_params=_cparams("arbitrary"),
        name="route",
    )(hb, wrt, rb)


def _plan_kernel(cnt_ref, ti_ref, rk_ref, dest_ref, bexp_ref, bval_ref, nused_ref, off_ref, *, tn, tmx, nb_pad):
    i = pl.program_id(0)

    @pl.when(i == 0)
    def _():
        cnt = cnt_ref[...]
        nblk = jnp.floor((cnt + (tmx - 1)) * (1.0 / tmx))
        r = lax.broadcasted_iota(I32, (N_EXPERTS, N_EXPERTS), 0)
        c = lax.broadcasted_iota(I32, (N_EXPERTS, N_EXPERTS), 1)
        lower = jnp.where(c < r, 1.0, 0.0).astype(BF16)
        bstart = _dot_exact_rhs(lower, nblk)
        bend = bstart + nblk
        off_ref[...] = bstart * tmx
        blk = lax.broadcasted_iota(I32, (N_EXPERTS, nb_pad), 1).astype(F32)
        done = jnp.where(bend[:, 0:1] <= blk, 1.0, 0.0)
        e_of = jnp.minimum(jnp.sum(done, axis=0, keepdims=True), N_EXPERTS - 1.0)
        bexp_ref[...] = e_of.astype(I32)
        eidx = lax.broadcasted_iota(I32, (N_EXPERTS, nb_pad), 0).astype(F32)
        mine = eidx == e_of
        pick = lambda col: jnp.sum(jnp.where(mine, col, 0.0), axis=0, keepdims=True)
        within = blk[0:1, :] - pick(bstart[:, 0:1])
        bval_ref[...] = jnp.clip(pick(cnt[:, 0:1]) - within * tmx, 0.0, float(tmx)).astype(I32)
        nused_ref[...] = jnp.max(bend, axis=0, keepdims=True).astype(I32)

    eidx = lax.broadcasted_iota(I32, (N_EXPERTS, tn), 0)
    off = off_ref[:, 0:1]
    for k in range(TOP_K):
        o = jnp.sum(jnp.where(eidx == ti_ref[k:k + 1, :], off, 0.0), axis=0, keepdims=True)
        dest_ref[k:k + 1, :] = o.astype(I32) + rk_ref[k:k + 1, :]


def _plan(cnt, ti, rk, *, tn, tmx, nb_pad):
    n = ti.shape[1]
    col = pl.BlockSpec((TOP_K, tn), lambda i: (0, i))
    return pl.pallas_call(
        functools.partial(_plan_kernel, tn=tn, tmx=tmx, nb_pad=nb_pad),
        grid=(n // tn,),
        in_specs=[_const_spec((N_EXPERTS, LANES)), col, col],
        out_specs=[col, _const_spec((1, nb_pad)), _const_spec((1, nb_pad)), _const_spec((1, LANES))],
        out_shape=[jax.ShapeDtypeStruct((TOP_K, n), I32),
                   jax.ShapeDtypeStruct((1, nb_pad), I32),
                   jax.ShapeDtypeStruct((1, nb_pad), I32),
                   jax.ShapeDtypeStruct((1, LANES), I32)],
        scratch_shapes=[pltpu.VMEM((N_EXPERTS, LANES), F32)],
        compiler_params=_cparams("arbitrary"),
        name="plan",
    )(cnt, ti, rk)


def _row_copy(src, src_row, dst, dst_row, sem):
    return pltpu.make_async_copy(src.at[pl.ds(pl.multiple_of(src_row * ROW_CHUNKS, ROW_CHUNKS), ROW_CHUNKS)],
                                 dst.at[pl.ds(pl.multiple_of(dst_row * ROW_CHUNKS, ROW_CHUNKS), ROW_CHUNKS)],
                                 sem)


def _dispatch_kernel(dest_ref, hc_ref, xs_ref, sem, *, tn):
    i = pl.program_id(0)

    def issue(t, carry):
        for k in range(TOP_K):
            _row_copy(hc_ref, i * tn + t, xs_ref, dest_ref[k, t], sem).start()
        return carry

    lax.fori_loop(0, tn, issue, 0)
    span = pl.ds(0, tn * TOP_K * ROW_CHUNKS)
    pltpu.make_async_copy(hc_ref.at[span], xs_ref.at[span], sem).wait()


def _dispatch(dest, hc, *, tn, rows):
    n = dest.shape[1]
    return pl.pallas_call(
        functools.partial(_dispatch_kernel, tn=tn),
        grid=(n // tn,),
        in_specs=[pl.BlockSpec((TOP_K, tn), lambda i: (0, i), memory_space=pltpu.SMEM),
                  pl.BlockSpec(memory_space=pl.ANY)],
        out_specs=pl.BlockSpec(memory_space=pl.ANY),
        out_shape=jax.ShapeDtypeStruct((rows * ROW_CHUNKS, LANES), F32),
        scratch_shapes=[pltpu.SemaphoreType.DMA(())],
        compiler_params=_cparams("arbitrary"),
        name="dispatch",
    )(dest, hc)


def _experts_kernel(bexp_ref, bval_ref, nused_ref, xs_ref, wg_ref, wu_ref, wd_ref, ys_ref, *, tmx):
    b = pl.program_id(0)

    @pl.when(b < nused_ref[0])
    def _():
        x = _load_row_chunks(xs_ref, tmx)
        valid = lax.broadcasted_iota(I32, (tmx, D_MODEL), 0) < bval_ref[b]
        xb = jnp.where(valid, x, 0.0).astype(BF16)
        g = _dot(xb, wg_ref[0].astype(BF16))
        u = _dot(xb, wu_ref[0].astype(BF16))
        y = _dot((_silu(g) * u).astype(BF16), wd_ref[0].astype(BF16))
        _store_row_chunks(ys_ref, y, tmx)


def _experts(bexp, bval, nused, xs, wg, wu, wd, *, tmx, nb):
    rows = xs.shape[0] // ROW_CHUNKS
    live = lambda b, nused: jnp.minimum(b, nused[0] - 1)
    xspec = pl.BlockSpec((tmx * ROW_CHUNKS, LANES), lambda b, be, bv, nu: (live(b, nu), 0))
    wspec = lambda shape: pl.BlockSpec((1,) + shape, lambda b, be, bv, nu: (be[live(b, nu)], 0, 0))
    grid_spec = pltpu.PrefetchScalarGridSpec(
        num_scalar_prefetch=3,
        grid=(nb,),
        in_specs=[xspec, wspec((D_MODEL, D_EXPERT)), wspec((D_MODEL, D_EXPERT)), wspec((D_EXPERT, D_MODEL))],
        out_specs=xspec,
    )
    return pl.pallas_call(
        functools.partial(_experts_kernel, tmx=tmx),
        grid_spec=grid_spec,
        out_shape=jax.ShapeDtypeStruct((rows * ROW_CHUNKS, LANES), F32),
        compiler_params=_cparams("arbitrary"),
        name="experts",
    )(bexp, bval, nused, xs, wg, wu, wd)


def _combine_kernel(dest_ref, ys_ref, tw_ref, xmid_ref, g2_ref, nf_ref, o_ref, gbuf_ref, sem, *, tn):
    def issue(t, carry):
        for k in range(TOP_K):
            _row_copy(ys_ref, dest_ref[k, t], gbuf_ref.at[k], t, sem).start()
        return carry

    lax.fori_loop(0, tn, issue, 0)
    for k in range(TOP_K):
        pltpu.make_async_copy(gbuf_ref.at[k], gbuf_ref.at[k], sem).wait()
    tw = tw_ref[...]
    routed = jnp.zeros((tn, D_MODEL), F32)
    for k in range(TOP_K):
        routed = routed + tw[:, k:k + 1] * _load_row_chunks(gbuf_ref, tn, lead=k)
    x2 = xmid_ref[...] + g2_ref[...] * routed
    ms = jnp.mean(x2 * x2, axis=-1, keepdims=True)
    o_ref[...] = x2 * lax.rsqrt(ms + EPS) * nf_ref[...]


def _combine(dest, ys, tw, xmid, g2, nf, *, tn, per_row):
    n = xmid.shape[0]
    tok = pl.BlockSpec((tn, D_MODEL), lambda i: (i, 0))
    return pl.pallas_call(
        functools.partial(_combine_kernel, tn=tn),
        grid=(n // tn,),
        in_specs=[pl.BlockSpec((TOP_K, tn), lambda i: (0, i), memory_space=pltpu.SMEM),
                  pl.BlockSpec(memory_space=pl.ANY),
                  pl.BlockSpec((tn, TOP_K), lambda i: (i, 0)),
                  tok, _mod_spec(per_row, tn), _const_spec((1, D_MODEL))],
        out_specs=tok,
        out_shape=jax.ShapeDtypeStruct((n, D_MODEL), F32),
        scratch_shapes=[pltpu.VMEM((TOP_K, tn * ROW_CHUNKS, LANES), F32), pltpu.SemaphoreType.DMA(())],
        compiler_params=_cparams("arbitrary"),
        name="combine_s" if per_row else "combine_p",
    )(dest, ys, tw, xmid, g2, nf)


TM_PROJ = 256
TQ_ATTN = 512
TM_POST = 256
TN_ROUTE = 512
TMX = 256
TN_COMBINE = 256


def _layer(xp, xs, cache_k, cache_v, cache_lf, state, page_table, c_p, c_s, w_ada, b_ada, norm_mix, w_in,
           forget_bias, conv_w, w_out, norm_ffn, w_router, router_bias, w_sh_gate, w_sh_up, w_sh_down,
           w_ex_gate, w_ex_up, w_ex_down, norm_final):
    s = xp.shape[0]
    nb_s = xs.shape[0]
    a_w = N_HEADS * HEAD_DIM
    col = 0
    def take(width):
        nonlocal col
        w = w_in[:, col:col + width]
        col += width
        return w
    wq, wk, wv = take(a_w), take(a_w), take(a_w)
    wf = take(N_HEADS)
    wgo, wgi, wxc, wga, wgc = (take(D_MODEL).astype(BF16) for _ in range(5))
    wf3 = jnp.pad(jnp.tile(wf, (1, 3)), ((0, 0), (0, LANES - 3 * N_HEADS))).astype(BF16)
    fb3 = jnp.pad(jnp.tile(forget_bias, 3), (0, LANES - 3 * N_HEADS)).reshape(1, LANES)
    wqb, wkb, wvb = wq.astype(BF16), wk.astype(BF16), wv.astype(BF16)
    cw8 = jnp.pad(conv_w, ((0, SUBLANES - CONV_W), (0, 0)))
    row = lambda v: v.reshape(1, D_MODEL)
    wob = w_out.astype(BF16)
    wsg, wsu, wsd = w_sh_gate.astype(BF16), w_sh_up.astype(BF16), w_sh_down.astype(BF16)

    n_c = 1 + nb_s
    c_all = jnp.pad(jnp.concatenate([c_p, c_s], axis=0), ((0, -n_c % SUBLANES), (0, 0)))
    mod = _adaln(c_all, w_ada, b_ada)
    sh1, sc1, g1, sh2, sc2, g2 = (mod[:, m * D_MODEL:(m + 1) * D_MODEL] for m in range(N_MOD))
    pr = lambda v: v[0:1]
    sm = lambda v: v[1:n_c]

    k_p, v_p, lf_p, kb, qt, vt4, fa = _qkv_prompt(xp, 1.0 + pr(sc1), pr(sh1), row(norm_mix), wqb.T, wkb, wvb,
                                                  wvb.T, wf3, fb3, tm=TM_PROJ)
    buf0 = jnp.zeros((SUBLANES, D_MODEL), F32)
    a_p, yc_p, tail_p = _gate_prompt(xp, 1.0 + pr(sc1), pr(sh1), row(norm_mix), wgo, wgi, wxc, wga, wgc, cw8,
                                     buf0, tm=TM_PROJ)
    ot = _attn_prompt(qt, kb, fa, vt4, tq=TQ_ATTN)
    xmid_p, hb_p, hc_p = _post(xp, ot, a_p, yc_p, pr(g1), 1.0 + pr(sc2), pr(sh2), pr(g2), row(norm_ffn), wob,
                               wsg, wsu, wsd, tm=TM_POST, per_row=False, transposed=True)

    q_s, k_s, v_s, lf3_s = _qkv_sample(xs, 1.0 + sm(sc1), sm(sh1), row(norm_mix), wqb, wkb, wvb, wf3, fb3)
    lf_s = lf3_s[:, :N_HEADS]
    a_s, yc_s, u_s = _gate_sample(xs, 1.0 + sm(sc1), sm(sh1), row(norm_mix), wgo, wgi, wxc, wga, wgc, cw8,
                                  state[:, 0], state[:, 1])
    y_s = _attn_sample(page_table, q_s, k_s, v_s, lf_s, cache_k, cache_v, cache_lf)
    xmid_s, hb_s, hc_s = _post(xs, y_s, a_s, yc_s, sm(g1), 1.0 + sm(sc2), sm(sh2), sm(g2), row(norm_ffn), wob,
                               wsg, wsu, wsd, tm=nb_s, per_row=True, transposed=False)

    n_tok = s + nb_s
    n_pad = -(-n_tok // TN_ROUTE) * TN_ROUTE
    pad = n_pad - n_tok
    hb_all = jnp.concatenate([hb_p, hb_s, jnp.zeros((pad, D_MODEL), BF16)], axis=0)
    hc_all = jnp.concatenate([hc_p, hc_s, jnp.zeros((pad * ROW_CHUNKS, LANES), F32)], axis=0)
    rb = jnp.broadcast_to(router_bias.reshape(N_EXPERTS, 1), (N_EXPERTS, LANES))
    ti, tw, rk, cnt = _route(hb_all, w_router.T.astype(BF16), rb, tn=TN_ROUTE)
    nb = n_pad * TOP_K // TMX + N_EXPERTS
    nb_pad = -(-nb // LANES) * LANES
    dest, bexp, bval, nused = _plan(cnt, ti, rk, tn=TN_ROUTE, tmx=TMX, nb_pad=nb_pad)
    xsorted = _dispatch(dest, hc_all, tn=TN_ROUTE, rows=nb * TMX)
    ys = _experts(bexp[0], bval[0], nused[0], xsorted, w_ex_gate, w_ex_up, w_ex_down, tmx=TMX, nb=nb)
    tw_t = tw.T
    y_p = _combine(dest[:, :s], ys, tw_t[:s], xmid_p, pr(g2), row(norm_final), tn=TN_COMBINE, per_row=False)
    y_s_out = _combine(dest[:, s:n_tok], ys, tw_t[s:n_tok], xmid_s, sm(g2), row(norm_final), tn=nb_s,
                       per_row=True)

    conv_p = tail_p[SUBLANES - (CONV_W - 1):]
    conv_s = jnp.stack([state[:, 1], u_s], axis=1)
    return y_p, y_s_out, k_p, v_p, lf_p, conv_p, k_s, v_s, lf_s, conv_s


def kernel(x_prompt, x_sample, cache_k, cache_v, cache_logf, state_conv, page_table, c_prompt, c_sample,
           w_ada, b_ada, norm_mix, w_in, forget_bias, conv_w, w_out, norm_ffn, w_router, router_bias,
           w_sh_gate, w_sh_up, w_sh_down, w_ex_gate, w_ex_up, w_ex_down, norm_final):
    depth = w_ada.shape[0]
    assert depth == 1 and x_prompt.shape[0] == 1 and x_sample.shape[1] == 1
    b, s, _ = x_prompt.shape
    bd = x_sample.shape[0]
    outs = _layer(x_prompt[0], x_sample[:, 0], cache_k[0], cache_v[0], cache_logf[0], state_conv[0], page_table,
                  c_prompt, c_sample, w_ada[0], b_ada[0], norm_mix[0], w_in[0], forget_bias[0], conv_w[0],
                  w_out[0], norm_ffn[0], w_router[0], router_bias[0], w_sh_gate[0], w_sh_up[0], w_sh_down[0],
                  w_ex_gate[0], w_ex_up[0], w_ex_down[0], norm_final)
    y_p, y_s, k_p, v_p, lf_p, conv_p, k_s, v_s, lf_s, conv_s = outs
    heads = lambda t, n: t.reshape(1, n, -1, N_HEADS, HEAD_DIM)
    return (y_p.reshape(b, s, D_MODEL), y_s.reshape(bd, 1, D_MODEL),
            heads(k_p, b), heads(v_p, b), lf_p.reshape(1, b, s, N_HEADS), conv_p.reshape(1, b, CONV_W - 1, D_MODEL),
            heads(k_s, bd), heads(v_s, bd), lf_s.reshape(1, bd, 1, N_HEADS),
            conv_s.reshape(1, bd, CONV_W - 1, D_MODEL))
```

```python
import functools

import jax
import jax.numpy as jnp
from jax import lax
from jax.experimental import pallas as pl
from jax.experimental.pallas import tpu as pltpu

F32 = jnp.float32
BF16 = jnp.bfloat16
I32 = jnp.int32

D_MODEL = 1024
N_HEADS = 16
HEAD_DIM = 64
PAGE = 128
CONV_W = 3
N_EXPERTS = 256
TOP_K = 8
N_GROUPS = 8
GROUP_SIZE = N_EXPERTS // N_GROUPS
TOPK_GROUPS = 4
D_EXPERT = 256
N_MOD = 6
EPS = 1e-6
ROUTE_SCALE = 2.5
LOG2E = 1.4426950408889634
QK_SCALE = HEAD_DIM ** -0.5

LANES = 128
SUBLANES = 8
ROW_CHUNKS = D_MODEL // LANES
VMEM_LIMIT = 48 * 1024 * 1024

NEG_INF = float("-inf")


def _cparams(*sem):
    return pltpu.CompilerParams(dimension_semantics=sem, vmem_limit_bytes=VMEM_LIMIT)


def _split3(x):
    hi = x.astype(BF16)
    r = x - hi.astype(F32)
    mid = r.astype(BF16)
    lo = (r - mid.astype(F32)).astype(BF16)
    return hi, mid, lo


def _dot(a, b):
    return jnp.dot(a, b, preferred_element_type=F32)


def _dot_nt(a, b):
    return lax.dot_general(a, b, (((1,), (1,)), ((), ())), preferred_element_type=F32)


def _dot_exact_rhs(a01, x):
    hi, mid, lo = _split3(x)
    return _dot(a01, hi) + _dot(a01, mid) + _dot(a01, lo)


def _dot_exact_lhs(x, b01):
    hi, mid, lo = _split3(x)
    return _dot(hi, b01) + _dot(mid, b01) + _dot(lo, b01)


def _silu(x):
    return x * jax.nn.sigmoid(x)


def _log_sigmoid(z):
    return jnp.minimum(z, 0.0) - jnp.log1p(jnp.exp(-jnp.abs(z)))


def _rms_mod(x, g, mul, shift):
    ms = jnp.mean(x * x, axis=-1, keepdims=True)
    return (x * lax.rsqrt(ms + EPS) * g) * mul + shift


def _mod_spec(per_row, tm):
    if per_row:
        return pl.BlockSpec((tm, D_MODEL), lambda i: (i, 0))
    return pl.BlockSpec((1, D_MODEL), lambda i: (0, 0))


def _const_spec(shape):
    nd = len(shape)
    return pl.BlockSpec(shape, lambda i: (0,) * nd)


def _adaln_kernel(c_ref, w_ref, b_ref, o_ref):
    a = _silu(c_ref[...]).astype(BF16)
    o_ref[...] = _dot(a, w_ref[...].astype(BF16)) + b_ref[...]


def _adaln(c_all, w_ada, b_ada):
    r = c_all.shape[0]
    n = w_ada.shape[1]
    tn = 1536
    return pl.pallas_call(
        _adaln_kernel,
        grid=(n // tn,),
        in_specs=[pl.BlockSpec((r, D_MODEL), lambda j: (0, 0)),
                  pl.BlockSpec((D_MODEL, tn), lambda j: (0, j)),
                  pl.BlockSpec((1, tn), lambda j: (0, j))],
        out_specs=pl.BlockSpec((r, tn), lambda j: (0, j)),
        out_shape=jax.ShapeDtypeStruct((r, n), F32),
        compiler_params=_cparams("arbitrary"),
        name="adaln",
    )(c_all, w_ada, b_ada.reshape(1, n))


def _qkv_prompt_kernel(x_ref, mul_ref, sh_ref, g_ref, wqt_ref, wk_ref, wv_ref, wvt_ref, wf_ref, fb_ref,
                       k_ref, v_ref, lf_ref, kb_ref, qt_ref, vt_ref, fa_ref, tri_ref, carry_ref, *, tm):
    i = pl.program_id(0)

    @pl.when(i == 0)
    def _():
        r = lax.broadcasted_iota(I32, (tm, tm), 0)
        c = lax.broadcasted_iota(I32, (tm, tm), 1)
        tri_ref[...] = jnp.where(c <= r, 1.0, 0.0).astype(BF16)
        carry_ref[...] = jnp.zeros_like(carry_ref)

    hb = _rms_mod(x_ref[...], g_ref[...], mul_ref[...], sh_ref[...]).astype(BF16)
    k = _dot(hb, wk_ref[...])
    k_ref[...] = k
    kb_ref[...] = k.astype(BF16)
    v_ref[...] = _dot(hb, wv_ref[...])
    qt_ref[...] = (_dot_nt(wqt_ref[...], hb) * (QK_SCALE * LOG2E)).astype(BF16)
    vt = _dot_nt(wvt_ref[...], hb).astype(BF16)
    for h in range(N_HEADS):
        vt_ref[h, 0] = vt[h * HEAD_DIM:(h + 1) * HEAD_DIM, :]
    lf = _log_sigmoid(_dot(hb, wf_ref[...]) + fb_ref[...])
    lf_ref[...] = lf[:, :N_HEADS]
    cum = _dot_exact_rhs(tri_ref[...], lf) + carry_ref[0:1, :]
    carry_ref[0:1, :] = cum[tm - 1:tm, :]
    hi, mid, lo = _split3(cum * (-LOG2E))
    lane = lax.broadcasted_iota(I32, (tm, LANES), 1)
    zero = jnp.zeros_like(hi)
    fa_ref[...] = jnp.where(lane < N_HEADS, hi,
                            jnp.where(lane < 2 * N_HEADS, mid, jnp.where(lane < 3 * N_HEADS, lo, zero)))


def _qkv_prompt(x, mul, sh, g, wqt, wk, wv, wvt, wf3, fb3, *, tm):
    s = x.shape[0]
    nt = s // tm
    row = pl.BlockSpec((tm, D_MODEL), lambda i: (i, 0))
    wspec = _const_spec((D_MODEL, D_MODEL))
    return pl.pallas_call(
        functools.partial(_qkv_prompt_kernel, tm=tm),
        grid=(nt,),
        in_specs=[row, _mod_spec(False, tm), _mod_spec(False, tm), _const_spec((1, D_MODEL)),
                  wspec, wspec, wspec, wspec, _const_spec((D_MODEL, LANES)), _const_spec((1, LANES))],
        out_specs=[row, row,
                   pl.BlockSpec((tm, N_HEADS), lambda i: (i, 0)),
                   row,
                   pl.BlockSpec((D_MODEL, tm), lambda i: (0, i)),
                   pl.BlockSpec((N_HEADS, 1, HEAD_DIM, tm), lambda i: (0, i, 0, 0)),
                   pl.BlockSpec((tm, LANES), lambda i: (i, 0))],
        out_shape=[jax.ShapeDtypeStruct((s, D_MODEL), F32),
                   jax.ShapeDtypeStruct((s, D_MODEL), F32),
                   jax.ShapeDtypeStruct((s, N_HEADS), F32),
                   jax.ShapeDtypeStruct((s, D_MODEL), BF16),
                   jax.ShapeDtypeStruct((D_MODEL, s), BF16),
                   jax.ShapeDtypeStruct((N_HEADS, nt, HEAD_DIM, tm), BF16),
                   jax.ShapeDtypeStruct((s, LANES), BF16)],
        scratch_shapes=[pltpu.VMEM((tm, tm), BF16), pltpu.VMEM((SUBLANES, LANES), F32)],
        compiler_params=_cparams("arbitrary"),
        name="qkv_prompt",
    )(x, mul, sh, g, wqt, wk, wv, wvt, wf3, fb3)


def _qkv_sample_kernel(x_ref, mul_ref, sh_ref, g_ref, wq_ref, wk_ref, wv_ref, wf_ref, fb_ref,
                       q_ref, k_ref, v_ref, lf_ref):
    hb = _rms_mod(x_ref[...], g_ref[...], mul_ref[...], sh_ref[...]).astype(BF16)
    q_ref[...] = _dot(hb, wq_ref[...]) * QK_SCALE
    k_ref[...] = _dot(hb, wk_ref[...])
    v_ref[...] = _dot(hb, wv_ref[...])
    lf_ref[...] = _log_sigmoid(_dot(hb, wf_ref[...]) + fb_ref[...])


def _qkv_sample(x, mul, sh, g, wq, wk, wv, wf3, fb3):
    n = x.shape[0]
    row = _const_spec((n, D_MODEL))
    wspec = _const_spec((D_MODEL, D_MODEL))
    return pl.pallas_call(
        _qkv_sample_kernel,
        grid=(1,),
        in_specs=[row, row, row, _const_spec((1, D_MODEL)), wspec, wspec, wspec,
                  _const_spec((D_MODEL, LANES)), _const_spec((1, LANES))],
        out_specs=[row, row, row, _const_spec((n, LANES))],
        out_shape=[jax.ShapeDtypeStruct((n, D_MODEL), F32)] * 3 + [jax.ShapeDtypeStruct((n, LANES), F32)],
        compiler_params=_cparams("arbitrary"),
        name="qkv_sample",
    )(x, mul, sh, g, wq, wk, wv, wf3, fb3)


def _gates(hb, wgo_ref, wgi_ref, wxc_ref, wga_ref, wgc_ref):
    go = _dot(hb, wgo_ref[...])
    u = _dot(hb, wgi_ref[...]) * _dot(hb, wxc_ref[...])
    a = jax.nn.sigmoid(_dot(hb, wga_ref[...]))
    gc = jax.nn.sigmoid(_dot(hb, wgc_ref[...])) * go
    return u, a, gc


def _gate_prompt_kernel(x_ref, mul_ref, sh_ref, g_ref, wgo_ref, wgi_ref, wxc_ref, wga_ref, wgc_ref,
                        cw_ref, buf_ref, a_ref, yc_ref, tail_ref, ubuf_ref, *, tm):
    i = pl.program_id(0)

    @pl.when(i == 0)
    def _():
        ubuf_ref[0:SUBLANES, :] = buf_ref[...]

    hb = _rms_mod(x_ref[...], g_ref[...], mul_ref[...], sh_ref[...]).astype(BF16)
    u, a, gc = _gates(hb, wgo_ref, wgi_ref, wxc_ref, wga_ref, wgc_ref)
    ubuf_ref[SUBLANES:SUBLANES + tm, :] = u
    conv = (ubuf_ref[SUBLANES - 2:SUBLANES - 2 + tm, :] * cw_ref[0:1, :]
            + ubuf_ref[SUBLANES - 1:SUBLANES - 1 + tm, :] * cw_ref[1:2, :]
            + u * cw_ref[2:3, :])
    a_ref[...] = a
    yc_ref[...] = gc * conv
    last = u[tm - SUBLANES:tm, :]
    ubuf_ref[0:SUBLANES, :] = last
    tail_ref[...] = last


def _gate_prompt(x, mul, sh, g, wgo, wgi, wxc, wga, wgc, cw8, buf8, *, tm):
    s = x.shape[0]
    row = pl.BlockSpec((tm, D_MODEL), lambda i: (i, 0))
    wspec = _const_spec((D_MODEL, D_MODEL))
    small = _const_spec((SUBLANES, D_MODEL))
    return pl.pallas_call(
        functools.partial(_gate_prompt_kernel, tm=tm),
        grid=(s // tm,),
        in_specs=[row, _mod_spec(False, tm), _mod_spec(False, tm), _const_spec((1, D_MODEL)),
                  wspec, wspec, wspec, wspec, wspec, small, small],
        out_specs=[row, row, small],
        out_shape=[jax.ShapeDtypeStruct((s, D_MODEL), F32),
                   jax.ShapeDtypeStruct((s, D_MODEL), F32),
                   jax.ShapeDtypeStruct((SUBLANES, D_MODEL), F32)],
        scratch_shapes=[pltpu.VMEM((tm + SUBLANES, D_MODEL), F32)],
        compiler_params=_cparams("arbitrary"),
        name="gate_prompt",
    )(x, mul, sh, g, wgo, wgi, wxc, wga, wgc, cw8, buf8)


def _gate_sample_kernel(x_ref, mul_ref, sh_ref, g_ref, wgo_ref, wgi_ref, wxc_ref, wga_ref, wgc_ref,
                        cw_ref, p2_ref, p1_ref, a_ref, yc_ref, u_ref):
    hb = _rms_mod(x_ref[...], g_ref[...], mul_ref[...], sh_ref[...]).astype(BF16)
    u, a, gc = _gates(hb, wgo_ref, wgi_ref, wxc_ref, wga_ref, wgc_ref)
    conv = p2_ref[...] * cw_ref[0:1, :] + p1_ref[...] * cw_ref[1:2, :] + u * cw_ref[2:3, :]
    a_ref[...] = a
    yc_ref[...] = gc * conv
    u_ref[...] = u


def _gate_sample(x, mul, sh, g, wgo, wgi, wxc, wga, wgc, cw8, p2, p1):
    n = x.shape[0]
    row = _const_spec((n, D_MODEL))
    wspec = _const_spec((D_MODEL, D_MODEL))
    return pl.pallas_call(
        _gate_sample_kernel,
        grid=(1,),
        in_specs=[row, row, row, _const_spec((1, D_MODEL)), wspec, wspec, wspec, wspec, wspec,
                  _const_spec((SUBLANES, D_MODEL)), row, row],
        out_specs=[row, row, row],
        out_shape=[jax.ShapeDtypeStruct((n, D_MODEL), F32)] * 3,
        compiler_params=_cparams("arbitrary"),
        name="gate_sample",
    )(x, mul, sh, g, wgo, wgi, wxc, wga, wgc, cw8, p2, p1)


def _attn_prompt_kernel(qt_ref, k_ref, fa_ref, vt_ref, o_ref, sa_ref, sb_ref, *, tq, tk):
    g = pl.program_id(0)
    i = pl.program_id(1)
    rows = lax.broadcasted_iota(I32, (2 * HEAD_DIM, tq), 0)
    qt = qt_ref[...].astype(F32)
    halves = []
    for hh in range(2):
        h = 2 * g + hh
        mine = (rows >= HEAD_DIM) if hh else (rows < HEAD_DIM)
        qm = jnp.where(mine, qt, 0.0).astype(BF16)
        ones = jnp.where((rows == h) | (rows == h + N_HEADS) | (rows == h + 2 * N_HEADS), 1.0, 0.0).astype(BF16)
        halves.append(jnp.concatenate([qm, ones], axis=0))
    qa = jnp.concatenate(halves, axis=1)

    def produce(kb, buf, masked):
        k0 = pl.multiple_of(kb * tk, tk)
        ka = jnp.concatenate([k_ref[pl.ds(k0, tk), :], fa_ref[pl.ds(k0, tk), :]], axis=1)
        s = _dot(ka, qa)
        if masked:
            kpos = k0 + lax.broadcasted_iota(I32, (tk, 2 * tq), 0)
            qpos = i * tq + (lax.broadcasted_iota(I32, (tk, 2 * tq), 1) & (tq - 1))
            s = jnp.where(kpos <= qpos, s, NEG_INF)
        buf[...] = s
        return jnp.max(s, axis=0, keepdims=True)

    def consume(kb, buf, top, carry):
        m, l, acc = carry
        m_new = jnp.maximum(m, top)
        alpha = jnp.exp2(m - m_new)
        p = jnp.exp2(buf[...] - m_new)
        l = alpha * l + jnp.sum(p, axis=0, keepdims=True)
        pb = p.astype(BF16)
        pv = [_dot(vt_ref[hh, kb], pb[:, hh * tq:(hh + 1) * tq]) for hh in range(2)]
        return m_new, l, alpha * acc + jnp.concatenate(pv, axis=1)

    assert tq == 2 * tk and tq & (tq - 1) == 0
    init = (jnp.full((1, 2 * tq), NEG_INF, F32), jnp.zeros((1, 2 * tq), F32),
            jnp.zeros((HEAD_DIM, 2 * tq), F32))
    d0 = 2 * i
    top_a = produce(d0, sa_ref, True)
    top_b = produce(d0 + 1, sb_ref, True)
    carry = consume(d0, sa_ref, top_a, init)

    def pair(u, state):
        top_b, m, l, acc = state
        top_a = produce(2 * u, sa_ref, False)
        carry = consume(jnp.where(u == 0, d0 + 1, 2 * u - 1), sb_ref, top_b, (m, l, acc))
        top_b = produce(2 * u + 1, sb_ref, False)
        return (top_b,) + consume(2 * u, sa_ref, top_a, carry)

    state = lax.fori_loop(0, i, pair, (top_b,) + carry)
    _, l, acc = consume(jnp.where(i == 0, d0 + 1, d0 - 1), sb_ref, state[0], state[1:])
    out = acc / l
    o_ref[0:HEAD_DIM, :] = out[:, :tq]
    o_ref[HEAD_DIM:, :] = out[:, tq:]


def _attn_prompt(qt, kb, fa, vt4, *, tq):
    s = kb.shape[0]
    nkb, tk = vt4.shape[1], vt4.shape[3]
    return pl.pallas_call(
        functools.partial(_attn_prompt_kernel, tq=tq, tk=tk),
        grid=(N_HEADS // 2, s // tq),
        in_specs=[pl.BlockSpec((2 * HEAD_DIM, tq), lambda g, i: (g, i)),
                  pl.BlockSpec((s, 2 * HEAD_DIM), lambda g, i: (0, g)),
                  pl.BlockSpec((s, LANES), lambda g, i: (0, 0)),
                  pl.BlockSpec((2, nkb, HEAD_DIM, tk), lambda g, i: (g, 0, 0, 0))],
        out_specs=pl.BlockSpec((2 * HEAD_DIM, tq), lambda g, i: (g, i)),
        out_shape=jax.ShapeDtypeStruct((D_MODEL, s), F32),
        scratch_shapes=[pltpu.VMEM((tk, 2 * tq), F32), pltpu.VMEM((tk, 2 * tq), F32)],
        compiler_params=_cparams("arbitrary", "arbitrary"),
        name="attn_prompt",
    )(qt, kb, fa, vt4)


def _attn_sample_kernel(pt_ref, q_ref, kn_ref, vn_ref, lfn_ref, sel_ref, selt_ref, *refs, pages_per_step):
    n = pages_per_step
    ck_refs, cv_refs, clf_refs = refs[:n], refs[n:2 * n], refs[2 * n:3 * n]
    o_ref, m_ref, l_ref, c_ref, acc_ref, qb_ref, tri_ref = refs[3 * n:]
    j = pl.program_id(1)
    tiles = HEAD_DIM // SUBLANES
    sel = sel_ref[...]

    def head_dots(get_rows):
        parts = []
        for h in range(N_HEADS):
            prod = get_rows(h) * qb_ref[h]
            parts.append(jnp.sum(prod.reshape(tiles, SUBLANES, PAGE), axis=0))
        return _dot_exact_rhs(sel, jnp.concatenate(parts, axis=0))

    @pl.when(j == 0)
    def _():
        lane = lax.broadcasted_iota(I32, (HEAD_DIM, PAGE), 1)
        for h in range(N_HEADS):
            qb_ref[h] = jnp.broadcast_to(q_ref[0, h], (HEAD_DIM, PAGE))
            acc_ref[h] = jnp.where(lane == 0, vn_ref[0, h], 0.0)
        m_ref[...] = head_dots(lambda h: jnp.broadcast_to(kn_ref[0, h], (HEAD_DIM, PAGE)))
        l_ref[...] = jnp.ones_like(l_ref)
        c_ref[...] = jnp.broadcast_to(lfn_ref[0], (N_HEADS, PAGE))
        r = lax.broadcasted_iota(I32, (PAGE, PAGE), 0)
        c = lax.broadcasted_iota(I32, (PAGE, PAGE), 1)
        tri_ref[...] = jnp.where(r > c, 1.0, 0.0).astype(BF16)

    ones = jnp.ones((PAGE, PAGE), BF16)
    newer = c_ref[...]
    scores = []
    for g in range(n):
        lf = clf_refs[g][0]
        suffix = _dot_exact_lhs(lf, tri_ref[...]) + newer
        newer = newer + _dot_exact_lhs(lf, ones)
        scores.append(head_dots(lambda h: ck_refs[g][0, h]) + suffix)
    c_ref[...] = newer
    m_old = m_ref[...]
    top = functools.reduce(jnp.maximum, scores)
    m_new = jnp.maximum(m_old, jnp.max(top, axis=1, keepdims=True))
    alpha = jnp.exp(m_old - m_new)
    probs = [jnp.exp(s - m_new) for s in scores]
    total = functools.reduce(lambda a, b: a + b, probs)
    l_ref[...] = alpha * l_ref[...] + jnp.sum(total, axis=1, keepdims=True)
    m_ref[...] = m_new
    spread = [_dot_exact_rhs(selt_ref[...], p) for p in probs]
    for h in range(N_HEADS):
        acc = acc_ref[h] * alpha[h:h + 1, :]
        for g in range(n):
            v = cv_refs[g][0, h].reshape(tiles, SUBLANES, PAGE)
            acc = acc + (v * spread[g][h * SUBLANES:(h + 1) * SUBLANES, :]).reshape(HEAD_DIM, PAGE)
        acc_ref[h] = acc

    @pl.when(j == pl.num_programs(1) - 1)
    def _():
        for h in range(N_HEADS):
            o_ref[0, h] = jnp.sum(acc_ref[h], axis=1, keepdims=True) / l_ref[h:h + 1, 0:1]


def _attn_sample(page_table, q, kn, vn, lfn, ck, cv, clf, *, pages_per_step):
    b, npages = page_table.shape
    n = pages_per_step
    col = lambda t: t.reshape(b, N_HEADS, HEAD_DIM, 1)
    per_b = pl.BlockSpec((1, N_HEADS, HEAD_DIM, 1), lambda bi, j, pt: (bi, 0, 0, 0))

    def page(g, four_d):
        pick = lambda bi, j, pt: pt[bi, npages - 1 - (j * n + g)]
        if four_d:
            return pl.BlockSpec((1, N_HEADS, HEAD_DIM, PAGE), lambda bi, j, pt: (pick(bi, j, pt), 0, 0, 0))
        return pl.BlockSpec((1, N_HEADS, PAGE), lambda bi, j, pt: (pick(bi, j, pt), 0, 0))

    hs = N_HEADS * SUBLANES
    sel = (jnp.arange(hs, dtype=I32)[None, :] // SUBLANES == jnp.arange(N_HEADS, dtype=I32)[:, None]).astype(BF16)
    cst = lambda shape: pl.BlockSpec(shape, lambda bi, j, pt: (0, 0))
    grid_spec = pltpu.PrefetchScalarGridSpec(
        num_scalar_prefetch=1,
        grid=(b, npages // n),
        in_specs=([per_b, per_b, per_b, pl.BlockSpec((1, N_HEADS, 1), lambda bi, j, pt: (bi, 0, 0)),
                   cst((N_HEADS, hs)), cst((hs, N_HEADS))]
                  + [page(g, True) for g in range(n)] + [page(g, True) for g in range(n)]
                  + [page(g, False) for g in range(n)]),
        out_specs=per_b,
        scratch_shapes=[pltpu.VMEM((N_HEADS, PAGE), F32),
                        pltpu.VMEM((N_HEADS, PAGE), F32),
                        pltpu.VMEM((N_HEADS, PAGE), F32),
                        pltpu.VMEM((N_HEADS, HEAD_DIM, PAGE), F32),
                        pltpu.VMEM((N_HEADS, HEAD_DIM, PAGE), F32),
                        pltpu.VMEM((PAGE, PAGE), BF16)],
    )
    out = pl.pallas_call(
        functools.partial(_attn_sample_kernel, pages_per_step=n),
        grid_spec=grid_spec,
        out_shape=jax.ShapeDtypeStruct((b, N_HEADS, HEAD_DIM, 1), F32),
        compiler_params=_cparams("arbitrary", "arbitrary"),
        name="attn_sample",
    )(page_table, col(q), col(kn), col(vn), lfn.reshape(b, N_HEADS, 1), sel, sel.T,
      *([ck] * n), *([cv] * n), *([clf] * n))
    return out.reshape(b, D_MODEL)


def _store_row_chunks(ref, val, rows):
    for c in range(ROW_CHUNKS):
        ref[pl.ds(c, rows, stride=ROW_CHUNKS), :] = val[:, c * LANES:(c + 1) * LANES]


def _load_row_chunks(ref, rows, lead=None):
    parts = []
    for c in range(ROW_CHUNKS):
        idx = (pl.ds(c, rows, stride=ROW_CHUNKS), slice(None))
        parts.append(ref[idx] if lead is None else ref[(lead,) + idx])
    return jnp.concatenate(parts, axis=1)


def _post_kernel(x_ref, y_ref, a_ref, yc_ref, g1_ref, mul_ref, sh_ref, g2_ref, nf_ref, wo_ref,
                 wsg_ref, wsu_ref, wsd_ref, xmid_ref, hb_ref, hc_ref, *, tm, transposed):
    y = y_ref[...].T if transposed else y_ref[...]
    merged = a_ref[...] * y + yc_ref[...]
    x1 = x_ref[...] + g1_ref[...] * _dot(merged.astype(BF16), wo_ref[...])
    h2 = _rms_mod(x1, nf_ref[...], mul_ref[...], sh_ref[...])
    hb = h2.astype(BF16)
    act = (_silu(_dot(hb, wsg_ref[...])) * _dot(hb, wsu_ref[...])).astype(BF16)
    xmid_ref[...] = x1 + g2_ref[...] * _dot(act, wsd_ref[...])
    hb_ref[...] = hb
    _store_row_chunks(hc_ref, h2, tm)


def _post(x, y, a, yc, g1, mul, sh, g2, nf, wo, wsg, wsu, wsd, *, tm, per_row, transposed):
    n = x.shape[0]
    row = pl.BlockSpec((tm, D_MODEL), lambda i: (i, 0))
    yspec = pl.BlockSpec((D_MODEL, tm), lambda i: (0, i)) if transposed else row
    mod = _mod_spec(per_row, tm)
    return pl.pallas_call(
        functools.partial(_post_kernel, tm=tm, transposed=transposed),
        grid=(n // tm,),
        in_specs=[row, yspec, row, row, mod, mod, mod, mod, _const_spec((1, D_MODEL)),
                  _const_spec((D_MODEL, D_MODEL)), _const_spec((D_MODEL, D_EXPERT)),
                  _const_spec((D_MODEL, D_EXPERT)), _const_spec((D_EXPERT, D_MODEL))],
        out_specs=[row, row, pl.BlockSpec((tm * ROW_CHUNKS, LANES), lambda i: (i, 0))],
        out_shape=[jax.ShapeDtypeStruct((n, D_MODEL), F32),
                   jax.ShapeDtypeStruct((n, D_MODEL), BF16),
                   jax.ShapeDtypeStruct((n * ROW_CHUNKS, LANES), F32)],
        compiler_params=_cparams("arbitrary"),
        name="post_t" if transposed else "post_n",
    )(x, y, a, yc, g1, mul, sh, g2, nf, wo, wsg, wsu, wsd)


def _route_kernel(h_ref, wrt_ref, rb_ref, ti_ref, tw_ref, rk_ref, cnt_ref, tri_ref, base_ref, *, tn):
    i = pl.program_id(0)

    @pl.when(i == 0)
    def _():
        r = lax.broadcasted_iota(I32, (tn, tn), 0)
        c = lax.broadcasted_iota(I32, (tn, tn), 1)
        tri_ref[...] = jnp.where(r < c, 1.0, 0.0).astype(BF16)
        base_ref[...] = jnp.zeros_like(base_ref)

    scores = jax.nn.sigmoid(_dot_nt(wrt_ref[...], h_ref[...]))
    biased = scores + rb_ref[:, 0:1]
    gidx = lax.broadcasted_iota(I32, (GROUP_SIZE, tn), 0)
    groups = [biased[g * GROUP_SIZE:(g + 1) * GROUP_SIZE, :] for g in range(N_GROUPS)]
    gs = []
    for grp in groups:
        m1 = jnp.max(grp, axis=0, keepdims=True)
        first = jnp.min(jnp.where(grp == m1, gidx, GROUP_SIZE), axis=0, keepdims=True)
        m2 = jnp.max(jnp.where(gidx == first, NEG_INF, grp), axis=0, keepdims=True)
        gs.append(m1 + m2)
    cands = []
    for g in range(N_GROUPS):
        beaten = jnp.zeros((1, tn), I32)
        for o in range(N_GROUPS):
            if o != g:
                ahead = (gs[o] >= gs[g]) if o < g else (gs[o] > gs[g])
                beaten = beaten + jnp.where(ahead, 1, 0)
        cands.append(jnp.where(beaten < TOPK_GROUPS, groups[g], NEG_INF))
    cand = jnp.concatenate(cands, axis=0)
    eidx = lax.broadcasted_iota(I32, (N_EXPERTS, tn), 0)
    sel = jnp.zeros((N_EXPERTS, tn), F32)
    picks, weights = [], []
    for _ in range(TOP_K):
        mk = jnp.max(cand, axis=0, keepdims=True)
        ik = jnp.min(jnp.where(cand == mk, eidx, N_EXPERTS), axis=0, keepdims=True)
        hit = eidx == ik
        picks.append(ik)
        weights.append(jnp.sum(jnp.where(hit, scores, 0.0), axis=0, keepdims=True))
        sel = jnp.where(hit, 1.0, sel)
        cand = jnp.where(hit, NEG_INF, cand)
    wsum = weights[0]
    for w in weights[1:]:
        wsum = wsum + w
    before = _dot(sel.astype(BF16), tri_ref[...]) + base_ref[:, 0:1]
    for k in range(TOP_K):
        ti_ref[k:k + 1, :] = picks[k]
        tw_ref[k:k + 1, :] = ROUTE_SCALE * weights[k] / wsum
        rank = jnp.sum(jnp.where(eidx == picks[k], before, 0.0), axis=0, keepdims=True)
        rk_ref[k:k + 1, :] = rank.astype(I32)
    base_ref[...] = base_ref[...] + jnp.sum(sel, axis=1, keepdims=True)
    cnt_ref[...] = base_ref[...]


def _route(hb, wrt, rb, *, tn):
    n = hb.shape[0]
    col = pl.BlockSpec((TOP_K, tn), lambda i: (0, i))
    return pl.pallas_call(
        functools.partial(_route_kernel, tn=tn),
        grid=(n // tn,),
        in_specs=[pl.BlockSpec((tn, D_MODEL), lambda i: (i, 0)),
                  _const_spec((N_EXPERTS, D_MODEL)), _const_spec((N_EXPERTS, LANES))],
        out_specs=[col, col, col, _const_spec((N_EXPERTS, LANES))],
        out_shape=[jax.ShapeDtypeStruct((TOP_K, n), I32),
                   jax.ShapeDtypeStruct((TOP_K, n), F32),
                   jax.ShapeDtypeStruct((TOP_K, n), I32),
                   jax.ShapeDtypeStruct((N_EXPERTS, LANES), F32)],
        scratch_shapes=[pltpu.VMEM((tn, tn), BF16), pltpu.VMEM((N_EXPERTS, LANES), F32)],
        compiler_params=_cparams("arbitrary"),
        name="route",
    )(hb, wrt, rb)


def _plan_kernel(cnt_ref, ti_ref, rk_ref, dest_ref, bexp_ref, bval_ref, nused_ref, off_ref, *, tn, tmx, nb_pad):
    i = pl.program_id(0)

    @pl.when(i == 0)
    def _():
        cnt = cnt_ref[...]
        nblk = jnp.floor((cnt + (tmx - 1)) * (1.0 / tmx))
        r = lax.broadcasted_iota(I32, (N_EXPERTS, N_EXPERTS), 0)
        c = lax.broadcasted_iota(I32, (N_EXPERTS, N_EXPERTS), 1)
        lower = jnp.where(c < r, 1.0, 0.0).astype(BF16)
        bstart = _dot_exact_rhs(lower, nblk)
        bend = bstart + nblk
        off_ref[...] = bstart * tmx
        blk = lax.broadcasted_iota(I32, (N_EXPERTS, nb_pad), 1).astype(F32)
        done = jnp.where(bend[:, 0:1] <= blk, 1.0, 0.0)
        e_of = jnp.minimum(jnp.sum(done, axis=0, keepdims=True), N_EXPERTS - 1.0)
        bexp_ref[...] = e_of.astype(I32)
        eidx = lax.broadcasted_iota(I32, (N_EXPERTS, nb_pad), 0).astype(F32)
        mine = eidx == e_of
        pick = lambda col: jnp.sum(jnp.where(mine, col, 0.0), axis=0, keepdims=True)
        within = blk[0:1, :] - pick(bstart[:, 0:1])
        bval_ref[...] = jnp.clip(pick(cnt[:, 0:1]) - within * tmx, 0.0, float(tmx)).astype(I32)
        nused_ref[...] = jnp.max(bend, axis=0, keepdims=True).astype(I32)

    eidx = lax.broadcasted_iota(I32, (N_EXPERTS, tn), 0)
    off = off_ref[:, 0:1]
    for k in range(TOP_K):
        o = jnp.sum(jnp.where(eidx == ti_ref[k:k + 1, :], off, 0.0), axis=0, keepdims=True)
        dest_ref[k:k + 1, :] = o.astype(I32) + rk_ref[k:k + 1, :]


def _plan(cnt, ti, rk, *, tn, tmx, nb_pad):
    n = ti.shape[1]
    col = pl.BlockSpec((TOP_K, tn), lambda i: (0, i))
    return pl.pallas_call(
        functools.partial(_plan_kernel, tn=tn, tmx=tmx, nb_pad=nb_pad),
        grid=(n // tn,),
        in_specs=[_const_spec((N_EXPERTS, LANES)), col, col],
        out_specs=[col, _const_spec((1, nb_pad)), _const_spec((1, nb_pad)), _const_spec((1, LANES))],
        out_shape=[jax.ShapeDtypeStruct((TOP_K, n), I32),
                   jax.ShapeDtypeStruct((1, nb_pad), I32),
                   jax.ShapeDtypeStruct((1, nb_pad), I32),
                   jax.ShapeDtypeStruct((1, LANES), I32)],
        scratch_shapes=[pltpu.VMEM((N_EXPERTS, LANES), F32)],
        compiler_params=_cparams("arbitrary"),
        name="plan",
    )(cnt, ti, rk)


def _row_copy(src, src_row, dst, dst_row, sem):
    return pltpu.make_async_copy(src.at[pl.ds(pl.multiple_of(src_row * ROW_CHUNKS, ROW_CHUNKS), ROW_CHUNKS)],
                                 dst.at[pl.ds(pl.multiple_of(dst_row * ROW_CHUNKS, ROW_CHUNKS), ROW_CHUNKS)],
                                 sem)


def _dispatch_kernel(dest_ref, hc_ref, xs_ref, sem, *, tn):
    def issue(t, carry):
        for k in range(TOP_K):
            _row_copy(hc_ref, t, xs_ref, dest_ref[k, t], sem).start()
        return carry

    lax.fori_loop(0, tn, issue, 0)
    for k in range(TOP_K):
        pltpu.make_async_copy(hc_ref, hc_ref, sem).wait()


def _dispatch(dest, hc, *, tn, rows):
    n = dest.shape[1]
    return pl.pallas_call(
        functools.partial(_dispatch_kernel, tn=tn),
        grid=(n // tn,),
        in_specs=[pl.BlockSpec((TOP_K, tn), lambda i: (0, i), memory_space=pltpu.SMEM),
                  pl.BlockSpec((tn * ROW_CHUNKS, LANES), lambda i: (i, 0))],
        out_specs=pl.BlockSpec(memory_space=pl.ANY),
        out_shape=jax.ShapeDtypeStruct((rows * ROW_CHUNKS, LANES), F32),
        scratch_shapes=[pltpu.SemaphoreType.DMA(())],
        compiler_params=_cparams("arbitrary"),
        name="dispatch",
    )(dest, hc)


def _experts_kernel(bexp_ref, bval_ref, nused_ref, xs_ref, wg_ref, wu_ref, wd_ref, ys_ref, *, tmx):
    b = pl.program_id(0)

    @pl.when(b < nused_ref[0])
    def _():
        x = _load_row_chunks(xs_ref, tmx)
        valid = lax.broadcasted_iota(I32, (tmx, D_MODEL), 0) < bval_ref[b]
        xb = jnp.where(valid, x, 0.0).astype(BF16)
        g = _dot(xb, wg_ref[0].astype(BF16))
        u = _dot(xb, wu_ref[0].astype(BF16))
        y = _dot((_silu(g) * u).astype(BF16), wd_ref[0].astype(BF16))
        _store_row_chunks(ys_ref, y, tmx)


def _experts(bexp, bval, nused, xs, wg, wu, wd, *, tmx, nb):
    rows = xs.shape[0] // ROW_CHUNKS
    live = lambda b, nused: jnp.minimum(b, nused[0] - 1)
    xspec = pl.BlockSpec((tmx * ROW_CHUNKS, LANES), lambda b, be, bv, nu: (live(b, nu), 0))
    wspec = lambda shape: pl.BlockSpec((1,) + shape, lambda b, be, bv, nu: (be[live(b, nu)], 0, 0))
    grid_spec = pltpu.PrefetchScalarGridSpec(
        num_scalar_prefetch=3,
        grid=(nb,),
        in_specs=[xspec, wspec((D_MODEL, D_EXPERT)), wspec((D_MODEL, D_EXPERT)), wspec((D_EXPERT, D_MODEL))],
        out_specs=xspec,
    )
    return pl.pallas_call(
        functools.partial(_experts_kernel, tmx=tmx),
        grid_spec=grid_spec,
        out_shape=jax.ShapeDtypeStruct((rows * ROW_CHUNKS, LANES), F32),
        compiler_params=_cparams("arbitrary"),
        name="experts",
    )(bexp, bval, nused, xs, wg, wu, wd)


def _combine_kernel(dest_ref, ys_ref, tw_ref, xmid_ref, g2_ref, nf_ref, o_ref, gbuf_ref, sem, *, tn):
    def issue(t, carry):
        for k in range(TOP_K):
            _row_copy(ys_ref, dest_ref[k, t], gbuf_ref.at[k], t, sem).start()
        return carry

    lax.fori_loop(0, tn, issue, 0)
    for k in range(TOP_K):
        pltpu.make_async_copy(gbuf_ref.at[k], gbuf_ref.at[k], sem).wait()
    tw = tw_ref[...]
    routed = jnp.zeros((tn, D_MODEL), F32)
    for k in range(TOP_K):
        routed = routed + tw[:, k:k + 1] * _load_row_chunks(gbuf_ref, tn, lead=k)
    x2 = xmid_ref[...] + g2_ref[...] * routed
    ms = jnp.mean(x2 * x2, axis=-1, keepdims=True)
    o_ref[...] = x2 * lax.rsqrt(ms + EPS) * nf_ref[...]


def _combine(dest, ys, tw, xmid, g2, nf, *, tn, per_row):
    n = xmid.shape[0]
    tok = pl.BlockSpec((tn, D_MODEL), lambda i: (i, 0))
    return pl.pallas_call(
        functools.partial(_combine_kernel, tn=tn),
        grid=(n // tn,),
        in_specs=[pl.BlockSpec((TOP_K, tn), lambda i: (0, i), memory_space=pltpu.SMEM),
                  pl.BlockSpec(memory_space=pl.ANY),
                  pl.BlockSpec((tn, TOP_K), lambda i: (i, 0)),
                  tok, _mod_spec(per_row, tn), _const_spec((1, D_MODEL))],
        out_specs=tok,
        out_shape=jax.ShapeDtypeStruct((n, D_MODEL), F32),
        scratch_shapes=[pltpu.VMEM((TOP_K, tn * ROW_CHUNKS, LANES), F32), pltpu.SemaphoreType.DMA(())],
        compiler_params=_cparams("arbitrary"),
        name="combine_s" if per_row else "combine_p",
    )(dest, ys, tw, xmid, g2, nf)


TM_PROJ = 256
TQ_ATTN = 512
PAGES_PER_STEP = 8
TM_POST = 256
TN_ROUTE = 512
TMX = 256
TN_COMBINE = 256


def _layer(xp, xs, cache_k, cache_v, cache_lf, state, page_table, c_p, c_s, w_ada, b_ada, norm_mix, w_in,
           forget_bias, conv_w, w_out, norm_ffn, w_router, router_bias, w_sh_gate, w_sh_up, w_sh_down,
           w_ex_gate, w_ex_up, w_ex_down, norm_final):
    s = xp.shape[0]
    nb_s = xs.shape[0]
    a_w = N_HEADS * HEAD_DIM
    col = 0
    def take(width):
        nonlocal col
        w = w_in[:, col:col + width]
        col += width
        return w
    wq, wk, wv = take(a_w), take(a_w), take(a_w)
    wf = take(N_HEADS)
    wgo, wgi, wxc, wga, wgc = (take(D_MODEL).astype(BF16) for _ in range(5))
    wf3 = jnp.pad(jnp.tile(wf, (1, 3)), ((0, 0), (0, LANES - 3 * N_HEADS))).astype(BF16)
    fb3 = jnp.pad(jnp.tile(forget_bias, 3), (0, LANES - 3 * N_HEADS)).reshape(1, LANES)
    wqb, wkb, wvb = wq.astype(BF16), wk.astype(BF16), wv.astype(BF16)
    cw8 = jnp.pad(conv_w, ((0, SUBLANES - CONV_W), (0, 0)))
    row = lambda v: v.reshape(1, D_MODEL)
    wob = w_out.astype(BF16)
    wsg, wsu, wsd = w_sh_gate.astype(BF16), w_sh_up.astype(BF16), w_sh_down.astype(BF16)

    n_c = 1 + nb_s
    c_all = jnp.pad(jnp.concatenate([c_p, c_s], axis=0), ((0, -n_c % SUBLANES), (0, 0)))
    mod = _adaln(c_all, w_ada, b_ada)
    sh1, sc1, g1, sh2, sc2, g2 = (mod[:, m * D_MODEL:(m + 1) * D_MODEL] for m in range(N_MOD))
    pr = lambda v: v[0:1]
    sm = lambda v: v[1:n_c]

    k_p, v_p, lf_p, kb, qt, vt4, fa = _qkv_prompt(xp, 1.0 + pr(sc1), pr(sh1), row(norm_mix), wqb.T, wkb, wvb,
                                                  wvb.T, wf3, fb3, tm=TM_PROJ)
    buf0 = jnp.zeros((SUBLANES, D_MODEL), F32)
    a_p, yc_p, tail_p = _gate_prompt(xp, 1.0 + pr(sc1), pr(sh1), row(norm_mix), wgo, wgi, wxc, wga, wgc, cw8,
                                     buf0, tm=TM_PROJ)
    ot = _attn_prompt(qt, kb, fa, vt4, tq=TQ_ATTN)
    xmid_p, hb_p, hc_p = _post(xp, ot, a_p, yc_p, pr(g1), 1.0 + pr(sc2), pr(sh2), pr(g2), row(norm_ffn), wob,
                               wsg, wsu, wsd, tm=TM_POST, per_row=False, transposed=True)

    q_s, k_s, v_s, lf3_s = _qkv_sample(xs, 1.0 + sm(sc1), sm(sh1), row(norm_mix), wqb, wkb, wvb, wf3, fb3)
    lf_s = lf3_s[:, :N_HEADS]
    a_s, yc_s, u_s = _gate_sample(xs, 1.0 + sm(sc1), sm(sh1), row(norm_mix), wgo, wgi, wxc, wga, wgc, cw8,
                                  state[:, 0], state[:, 1])
    y_s = _attn_sample(page_table, q_s, k_s, v_s, lf_s, jnp.transpose(cache_k, (0, 2, 3, 1)),
                       jnp.transpose(cache_v, (0, 2, 3, 1)), jnp.transpose(cache_lf, (0, 2, 1)),
                       pages_per_step=PAGES_PER_STEP)
    xmid_s, hb_s, hc_s = _post(xs, y_s, a_s, yc_s, sm(g1), 1.0 + sm(sc2), sm(sh2), sm(g2), row(norm_ffn), wob,
                               wsg, wsu, wsd, tm=nb_s, per_row=True, transposed=False)

    n_tok = s + nb_s
    n_pad = -(-n_tok // TN_ROUTE) * TN_ROUTE
    pad = n_pad - n_tok
    hb_all = jnp.concatenate([hb_p, hb_s, jnp.zeros((pad, D_MODEL), BF16)], axis=0)
    hc_all = jnp.concatenate([hc_p, hc_s, jnp.zeros((pad * ROW_CHUNKS, LANES), F32)], axis=0)
    rb = jnp.broadcast_to(router_bias.reshape(N_EXPERTS, 1), (N_EXPERTS, LANES))
    ti, tw, rk, cnt = _route(hb_all, w_router.T.astype(BF16), rb, tn=TN_ROUTE)
    nb = n_pad * TOP_K // TMX + N_EXPERTS
    nb_pad = -(-nb // LANES) * LANES
    dest, bexp, bval, nused = _plan(cnt, ti, rk, tn=TN_ROUTE, tmx=TMX, nb_pad=nb_pad)
    xsorted = _dispatch(dest, hc_all, tn=TN_ROUTE, rows=nb * TMX)
    ys = _experts(bexp[0], bval[0], nused[0], xsorted, w_ex_gate, w_ex_up, w_ex_down, tmx=TMX, nb=nb)
    tw_t = tw.T
    y_p = _combine(dest[:, :s], ys, tw_t[:s], xmid_p, pr(g2), row(norm_final), tn=TN_COMBINE, per_row=False)
    y_s_out = _combine(dest[:, s:n_tok], ys, tw_t[s:n_tok], xmid_s, sm(g2), row(norm_final), tn=nb_s,
                       per_row=True)

    conv_p = tail_p[SUBLANES - (CONV_W - 1):]
    conv_s = jnp.stack([state[:, 1], u_s], axis=1)
    return y_p, y_s_out, k_p, v_p, lf_p, conv_p, k_s, v_s, lf_s, conv_s


def kernel(x_prompt, x_sample, cache_k, cache_v, cache_logf, state_conv, page_table, c_prompt, c_sample,
           w_ada, b_ada, norm_mix, w_in, forget_bias, conv_w, w_out, norm_ffn, w_router, router_bias,
           w_sh_gate, w_sh_up, w_sh_down, w_ex_gate, w_ex_up, w_ex_down, norm_final):
    depth = w_ada.shape[0]
    assert depth == 1 and x_prompt.shape[0] == 1 and x_sample.shape[1] == 1
    b, s, _ = x_prompt.shape
    bd = x_sample.shape[0]
    outs = _layer(x_prompt[0], x_sample[:, 0], cache_k[0], cache_v[0], cache_logf[0], state_conv[0], page_table,
                  c_prompt, c_sample, w_ada[0], b_ada[0], norm_mix[0], w_in[0], forget_bias[0], conv_w[0],
                  w_out[0], norm_ffn[0], w_router[0], router_bias[0], w_sh_gate[0], w_sh_up[0], w_sh_down[0],
                  w_ex_gate[0], w_ex_up[0], w_ex_down[0], norm_final)
    y_p, y_s, k_p, v_p, lf_p, conv_p, k_s, v_s, lf_s, conv_s = outs
    heads = lambda t, n: t.reshape(1, n, -1, N_HEADS, HEAD_DIM)
    return (y_p.reshape(b, s, D_MODEL), y_s.reshape(bd, 1, D_MODEL),
            heads(k_p, b), heads(v_p, b), lf_p.reshape(1, b, s, N_HEADS), conv_p.reshape(1, b, CONV_W - 1, D_MODEL),
            heads(k_s, bd), heads(v_s, bd), lf_s.reshape(1, bd, 1, N_HEADS),
            conv_s.reshape(1, bd, CONV_W - 1, D_MODEL))
```
